```python
import math
import jax
import jax.numpy as jnp
from jax import lax
import numpy as np

D_MODEL = 1024
BATCH = 32
SEQ = 256
DEPTH = 4
DEC_BATCH = 4
DEC_SEQ = 4096
PAST_LEN = 256

GRID_W = 64
BRANCH_WIDTH = 512
N_BRANCH = 3

A_DK = 128
A_DV = 128
A_HEADS = BRANCH_WIDTH // A_DV
A_CHUNK = 16

B_WIDTH = BRANCH_WIDTH
HY_BANDS = 16
HY_EMB = 1 + 2 * HY_BANDS
HY_FFN = 64

C_HEAD_DIM = 64
C_HEADS = BRANCH_WIDTH // C_HEAD_DIM
C_WIDTH = BRANCH_WIDTH
C_DECAY_LORA = 64
C_AAA_LORA = 64
C_GATE_LORA = 128

A_COLS = 3 * A_HEADS * A_DK + A_HEADS * A_DV + BRANCH_WIDTH
B_COLS = 3 * B_WIDTH
C_COLS = 3 * C_WIDTH + 2 * C_DECAY_LORA + 2 * C_AAA_LORA + C_GATE_LORA
IN_COLS = A_COLS + B_COLS + C_COLS + N_BRANCH * D_MODEL

P_HEADS = 8
P_NKEYS = 128
P_EXPERTS = P_NKEYS * P_NKEYS
P_TOPK = 16
P_KEY_DIM = 128
P_BLOCK = 128

EPS = 1e-6
RWKV_GN_EPS = 64e-5

kernel_name = 'hybrid_flow_prefix_trunk_step'


def rms_norm(x, gain):
    xf = x.astype(jnp.float32)
    return xf * lax.rsqrt(jnp.mean(xf * xf, axis=-1, keepdims=True) + EPS) * gain


def adaln(cvec, w, b):
    m = jax.nn.silu(cvec) @ w + b
    return jnp.split(m[..., None, :], 6, axis=-1)


def shift_prev(x):
    return jnp.pad(x, ((0, 0), (1, 0), (0, 0)))[:, :-1]


def shift_next(x):
    return jnp.pad(x, ((0, 0), (0, 1), (0, 0)))[:, 1:]


def neighbour_mean(x, latent):
    if not latent:
        return 0.5 * (shift_prev(x) + shift_next(x))
    B, T, C = x.shape
    rows = T // GRID_W
    gp = jnp.pad(x.reshape(B, rows, GRID_W, C), ((0, 0), (1, 1), (1, 1), (0, 0)))
    s = gp[:, :-2, 1:-1] + gp[:, 2:, 1:-1] + gp[:, 1:-1, :-2] + gp[:, 1:-1, 2:]
    return (0.25 * s).reshape(B, T, C)


def hgrn2_chunkwise(q, k, v, logf, S0):
    B, T, H, K = q.shape
    V = v.shape[-1]
    C = A_CHUNK
    N = T // C
    q, k, logf = (a.reshape(B, N, C, H, K) for a in (q, k, logf))
    v = v.reshape(B, N, C, H, V)
    b = jnp.cumsum(logf.astype(jnp.float32), axis=2)
    causal = jnp.tril(jnp.ones((C, C), dtype=bool))[:, :, None, None]
    diff = b[:, :, :, None] - b[:, :, None, :]
    decay = jnp.exp(jnp.where(causal, diff, -jnp.inf))
    attn = jnp.einsum('bnthk,bnshk,bntshk->bnhts', q, k, decay)
    o_intra = jnp.einsum('bnhts,bnshv->bnthv', attn, v)
    b_last = b[:, :, -1]
    ds = jnp.einsum('bnshk,bnshv->bnhkv', k * jnp.exp(b_last[:, :, None] - b), v)

    def step(S, inp):
        dec, dS = inp
        return S * dec[..., None] + dS, S

    S_fin, S_start = lax.scan(step, S0.astype(jnp.float32),
                              (jnp.moveaxis(jnp.exp(b_last), 1, 0), jnp.moveaxis(ds, 1, 0)))
    o_inter = jnp.einsum('bnthk,bnhkv->bnthv', q * jnp.exp(b), jnp.moveaxis(S_start, 0, 1))
    return (o_intra + o_inter).reshape(B, T, H, V), S_fin


def hgrn2_branch(pa, S0, lp):
    B, T, _ = pa.shape
    hk = A_HEADS * A_DK
    q, f_fwd, f_bwd, i, g = jnp.split(pa, [hk, 2 * hk, 3 * hk, 3 * hk + A_HEADS * A_DV], axis=-1)
    q = jax.nn.silu(q).reshape(B, T, A_HEADS, A_DK)
    i = i.reshape(B, T, A_HEADS, A_DV)
    outs, finals = [], []
    for d, fz in enumerate((f_fwd, f_bwd)):
        lb = lp['lb'][d]
        f = lb + (1.0 - lb) * jax.nn.sigmoid(fz.astype(jnp.float32))
        seq = (q, (1.0 - f).reshape(B, T, A_HEADS, A_DK), i, jnp.log(f).reshape(B, T, A_HEADS, A_DK))
        if d == 1:
            seq = tuple(z[:, ::-1] for z in seq)
        o, S = hgrn2_chunkwise(*seq, S0[:, d])
        outs.append(o if d == 0 else o[:, ::-1])
        finals.append(S)
    o = rms_norm(outs[0] + outs[1], lp['hgrn_norm']).reshape(B, T, BRANCH_WIDTH)
    return o * jax.nn.silu(g), jnp.stack(finals, axis=1)


def hyena_kernel_freq(L, lp):
    t = jnp.linspace(0.0, 1.0, L, dtype=jnp.float32)[:, None]
    bands = jnp.linspace(1e-4, HY_BANDS - 1, HY_BANDS, dtype=jnp.float32)[None, :]
    ang = (2 * math.pi / L) * jnp.arange(L, dtype=jnp.float32)[:, None] * bands
    z = jnp.concatenate([t, jnp.cos(ang), -jnp.sin(ang)], axis=-1)
    hf = jnp.sin(lp['hy_freq1'] * (z @ lp['hy_w1'] + lp['hy_b1']))
    hf = jnp.sin(lp['hy_freq2'] * (hf @ lp['hy_w2'] + lp['hy_b2']))
    hf = (hf @ lp['hy_w3']).reshape(L, 2, B_WIDTH) * jnp.exp(-t * jnp.abs(lp['hy_delta']))[:, None, :]
    kern = jnp.concatenate([hf[:, 0], jnp.zeros((1, B_WIDTH), hf.dtype), hf[:0:-1, 1]], axis=0)
    kern = kern / jnp.sum(jnp.abs(kern), axis=0, keepdims=True)
    return jnp.fft.rfft(kern.astype(jnp.float32), axis=0)


def hyena_branch(pb, lp):
    B, L, _ = pb.shape
    cw = lp['hy_conv_w']
    u = shift_prev(pb) * cw[0] + pb * cw[1] + shift_next(pb) * cw[2] + lp['hy_conv_b']
    x0, x1, v = jnp.split(u, 3, axis=-1)
    z = (v * x1).astype(jnp.float32)
    y = jnp.fft.irfft(jnp.fft.rfft(z, n=2 * L, axis=1) * hyena_kernel_freq(L, lp), n=2 * L, axis=1)[:, :L]
    return x0 * (y + z * lp['hy_bias'])


def rwkv7_scan(seq, S0):
    def step(S, inp):
        r_t, w_t, k_t, v_t, a_t, b_t = inp
        sa = jnp.einsum('bhvk,bhk->bhv', S, a_t)
        S = S * w_t[:, :, None, :] + sa[..., None] * b_t[:, :, None, :] + v_t[..., None] * k_t[:, :, None, :]
        return S, jnp.einsum('bhvk,bhk->bhv', S, r_t)

    S, ys = lax.scan(step, S0.astype(jnp.float32), tuple(jnp.swapaxes(z, 0, 1) for z in seq))
    return jnp.swapaxes(ys, 0, 1), S


def rwkv7_branch(pc, S0, lp, latent):
    B, T, _ = pc.shape
    pc = pc + (neighbour_mean(pc, latent) - pc) * lp['rw_mu']
    cut = [C_WIDTH, 2 * C_WIDTH, 3 * C_WIDTH, 3 * C_WIDTH + 2 * C_DECAY_LORA,
           3 * C_WIDTH + 2 * C_DECAY_LORA + 2 * C_AAA_LORA]
    r, k, v, wd, ad, gd = jnp.split(pc, cut, axis=-1)
    heads = lambda z: z.reshape(B, T, C_HEADS, C_HEAD_DIM)
    kk = heads(k * lp['rw_kk']).astype(jnp.float32)
    kk = kk * lax.rsqrt(jnp.sum(kk * kk, axis=-1, keepdims=True) + 1e-12)
    g = jax.nn.sigmoid(gd) @ lp['rw_g2']
    wd = wd.reshape(B, T, 2, C_DECAY_LORA)
    ad = ad.reshape(B, T, 2, C_AAA_LORA)
    ys, finals = [], []
    for d in range(2):
        w = -jax.nn.softplus(-(lp['rw_w0'][d] + jnp.tanh(wd[:, :, d]) @ lp['rw_w2'][d])) - 0.5
        decay = jnp.exp(-jnp.exp(w.astype(jnp.float32)))
        a = jax.nn.sigmoid(lp['rw_a0'][d] + ad[:, :, d] @ lp['rw_a2'][d])
        kd = k * (1 + (a - 1) * lp['rw_ka'])
        seq = (heads(r), heads(decay), heads(kd), heads(v), -kk, kk * heads(a))
        if d == 1:
            seq = tuple(z[:, ::-1] for z in seq)
        y, S = rwkv7_scan(seq, S0[:, d])
        ys.append(y if d == 0 else y[:, ::-1])
        finals.append(S)
    y = (ys[0] + ys[1]).astype(jnp.float32)
    mu = jnp.mean(y, axis=-1, keepdims=True)
    var = jnp.mean(jnp.square(y - mu), axis=-1, keepdims=True)
    y = ((y - mu) * lax.rsqrt(var + RWKV_GN_EPS)).reshape(B, T, C_WIDTH) * lp['rw_ln_g'] + lp['rw_ln_b']
    bonus = jnp.sum(heads(r) * heads(k) * lp['rw_rk'], axis=-1, keepdims=True) * heads(v)
    return (y + bonus.reshape(B, T, C_WIDTH)) * g, jnp.stack(finals, axis=1)


def token_mixers(h, lp, S_a0, S_c0, latent):
    proj = h @ lp['w_in']
    pa, pb, pc, pg = jnp.split(proj, [A_COLS, A_COLS + B_COLS, A_COLS + B_COLS + C_COLS], axis=-1)
    ya, S_a = hgrn2_branch(pa, S_a0, lp)
    yb = hyena_branch(pb, lp)
    yc, S_c = rwkv7_branch(pc, S_c0, lp, latent)
    ga, gb, gc = jnp.split(jax.nn.sigmoid(pg), N_BRANCH, axis=-1)
    wb = lp['w_branch']
    merged = ga * (ya @ wb[0]) + gb * (yb @ wb[1]) + gc * (yc @ wb[2])
    return merged @ lp['w_out'], S_a, S_c


def peer_ffn(h, lp):
    B, T, D = h.shape
    u_tab, v_tab = lp['p_u'], lp['p_v']

    def block(hb):
        q = (hb @ lp['p_wq']).reshape(P_BLOCK, P_HEADS, 2, P_KEY_DIM // 2)
        s1 = jnp.einsum('thd,nd->thn', q[:, :, 0], lp['p_key1'])
        s2 = jnp.einsum('thd,nd->thn', q[:, :, 1], lp['p_key2'])
        v1, i1 = lax.top_k(s1, P_TOPK)
        v2, i2 = lax.top_k(s2, P_TOPK)
        cand = (v1[..., :, None] + v2[..., None, :]).reshape(P_BLOCK, P_HEADS, P_TOPK * P_TOPK)
        sc, ci = lax.top_k(cand, P_TOPK)
        expert = (jnp.take_along_axis(i1, ci // P_TOPK, axis=-1) * P_NKEYS
                  + jnp.take_along_axis(i2, ci % P_TOPK, axis=-1))
        gate = jax.nn.softmax(sc.astype(jnp.float32), axis=-1)
        act = jax.nn.gelu(jnp.einsum('td,thkd->thk', hb, u_tab[expert]))
        return jnp.einsum('thk,thkd->td', gate * act, v_tab[expert])

    return lax.map(block, h.reshape(-1, P_BLOCK, D)).reshape(B, T, D)


def trunk_layer(x, mod, lp, S_a0, S_c0, latent):
    sh1, sc1, g1, sh2, sc2, g2 = mod
    h = rms_norm(x, lp['norm1']) * (1 + sc1) + sh1
    mix, S_a, S_c = token_mixers(h, lp, S_a0, S_c0, latent)
    x = x + g1 * mix
    h = rms_norm(x, lp['norm2']) * (1 + sc2) + sh2
    x = x + g2 * peer_ffn(h, lp)
    return x, S_a, S_c


def setup_inputs(seed: int = 0) -> dict:
    key = jax.random.key(seed)
    ks = iter(jax.random.split(key, 64))
    nrm = lambda shape, scale: jax.random.normal(next(ks), shape, jnp.float32) * scale
    unif = lambda shape, lo, hi: jax.random.uniform(next(ks), shape, jnp.float32, lo, hi)
    L, D = DEPTH, D_MODEL
    return {
        'x_prompt': nrm((BATCH, SEQ, D), 1.0),
        'x_sample': nrm((DEC_BATCH, DEC_SEQ, D), 1.0),
        'state_hgrn': nrm((DEC_BATCH, DEPTH, 2, A_HEADS, A_DK, A_DV), 0.3),
        'state_rwkv': nrm((DEC_BATCH, DEPTH, 2, C_HEADS, C_HEAD_DIM, C_HEAD_DIM), 0.1),
        'c': nrm((DEC_BATCH, D), 1.0),
        'c_ctx': nrm((D,), 1.0),
        'w_ada': nrm((L, D, 6 * D), 0.5 * D ** -0.5),
        'b_ada': nrm((L, 6 * D), 0.02),
        'norm1': 1.0 + nrm((L, D), 0.02),
        'norm2': 1.0 + nrm((L, D), 0.02),
        'w_in': nrm((L, D, IN_COLS), D ** -0.5),
        'hgrn_lb_logits': nrm((2, L, A_HEADS * A_DK), 0.5),
        'hgrn_norm': 1.0 + nrm((L, A_DV), 0.02),
        'hy_conv_w': nrm((L, 3, B_COLS), 0.5),
        'hy_conv_b': nrm((L, B_COLS), 0.02),
        'hy_w1': nrm((L, HY_EMB, HY_FFN), HY_EMB ** -0.5),
        'hy_b1': nrm((L, HY_FFN), 0.1),
        'hy_freq1': 1.0 + nrm((L, HY_FFN), 0.05),
        'hy_w2': nrm((L, HY_FFN, HY_FFN), HY_FFN ** -0.5),
        'hy_b2': nrm((L, HY_FFN), 0.1),
        'hy_freq2': 1.0 + nrm((L, HY_FFN), 0.05),
        'hy_w3': nrm((L, HY_FFN, 2 * B_WIDTH), HY_FFN ** -0.5),
        'hy_delta': unif((L, B_WIDTH), 3.07, 15.35),
        'hy_bias': nrm((L, B_WIDTH), 0.5),
        'rw_mu': unif((L, C_COLS), 0.0, 1.0),
        'rw_w0': unif((L, 2, C_WIDTH), -6.5, -1.5),
        'rw_w2': nrm((L, 2, C_DECAY_LORA, C_WIDTH), 0.1),
        'rw_a0': nrm((L, 2, C_WIDTH), 0.1),
        'rw_a2': nrm((L, 2, C_AAA_LORA, C_WIDTH), 0.1),
        'rw_g2': nrm((L, C_GATE_LORA, C_WIDTH), C_GATE_LORA ** -0.5),
        'rw_kk': 0.85 + nrm((L, C_WIDTH), 0.05),
        'rw_ka': 1.0 + nrm((L, C_WIDTH), 0.05),
        'rw_rk': nrm((L, C_HEADS, C_HEAD_DIM), 0.1),
        'rw_ln_g': 1.0 + nrm((L, C_WIDTH), 0.02),
        'rw_ln_b': nrm((L, C_WIDTH), 0.02),
        'w_branch': nrm((L, N_BRANCH, BRANCH_WIDTH, D), BRANCH_WIDTH ** -0.5),
        'w_out': nrm((L, D, D), D ** -0.5),
        'p_wq': nrm((L, D, P_HEADS * P_KEY_DIM), D ** -0.5),
        'p_key1': nrm((L, P_NKEYS, P_KEY_DIM // 2), (P_KEY_DIM // 2) ** -0.5),
        'p_key2': nrm((L, P_NKEYS, P_KEY_DIM // 2), (P_KEY_DIM // 2) ** -0.5),
        'p_u': nrm((L, P_EXPERTS, D), D ** -0.5),
        'p_v': nrm((L, P_EXPERTS, D), 0.5),
        'final_norm': 1.0 + nrm((D,), 0.02),
    }


def reference(x_prompt, x_sample, state_hgrn, state_rwkv, c, c_ctx, w_ada, b_ada, norm1, norm2, w_in,
              hgrn_lb_logits, hgrn_norm, hy_conv_w, hy_conv_b, hy_w1, hy_b1, hy_freq1, hy_w2, hy_b2,
              hy_freq2, hy_w3, hy_delta, hy_bias, rw_mu, rw_w0, rw_w2, rw_a0, rw_a2, rw_g2, rw_kk, rw_ka,
              rw_rk, rw_ln_g, rw_ln_b, w_branch, w_out, p_wq, p_key1, p_key2, p_u, p_v, final_norm):
    lb_p = jax.nn.softmax(hgrn_lb_logits.astype(jnp.float32), axis=1)
    lb = jnp.cumsum(lb_p, axis=1)
    lb = lb - lb[:, :1]
    n_prompt = x_prompt.shape[0]
    zero_a = jnp.zeros((n_prompt, 2, A_HEADS, A_DK, A_DV), jnp.float32)
    zero_c = jnp.zeros((n_prompt, 2, C_HEADS, C_HEAD_DIM, C_HEAD_DIM), jnp.float32)
    y_p, y_s = x_prompt, x_sample
    new_a, new_c = [], []
    for l in range(DEPTH):
        lp = {
            'norm1': norm1[l], 'norm2': norm2[l], 'w_in': w_in[l], 'hgrn_norm': hgrn_norm[l], 'lb': lb[:, l],
            'hy_conv_w': hy_conv_w[l], 'hy_conv_b': hy_conv_b[l], 'hy_w1': hy_w1[l], 'hy_b1': hy_b1[l],
            'hy_freq1': hy_freq1[l], 'hy_w2': hy_w2[l], 'hy_b2': hy_b2[l], 'hy_freq2': hy_freq2[l],
            'hy_w3': hy_w3[l], 'hy_delta': hy_delta[l], 'hy_bias': hy_bias[l],
            'rw_mu': rw_mu[l], 'rw_w0': rw_w0[l], 'rw_w2': rw_w2[l], 'rw_a0': rw_a0[l], 'rw_a2': rw_a2[l],
            'rw_g2': rw_g2[l], 'rw_kk': rw_kk[l], 'rw_ka': rw_ka[l], 'rw_rk': rw_rk[l],
            'rw_ln_g': rw_ln_g[l], 'rw_ln_b': rw_ln_b[l], 'w_branch': w_branch[l], 'w_out': w_out[l],
            'p_wq': p_wq[l], 'p_key1': p_key1[l], 'p_key2': p_key2[l], 'p_u': p_u[l], 'p_v': p_v[l],
        }
        mod_ctx = adaln(c_ctx, w_ada[l], b_ada[l])
        mod_lat = adaln(c, w_ada[l], b_ada[l])
        y_p, s_a, s_c = trunk_layer(y_p, mod_ctx, lp, zero_a, zero_c, False)
        y_s, _, _ = trunk_layer(y_s, mod_lat, lp, state_hgrn[:, l], state_rwkv[:, l], True)
        new_a.append(s_a)
        new_c.append(s_c)
    y_prompt = rms_norm(y_p, final_norm)
    y_sample = rms_norm(y_s, final_norm)
    new_state_hgrn = jnp.stack(new_a, axis=1)
    new_state_rwkv = jnp.stack(new_c, axis=1)
    return (y_prompt, y_sample, new_state_hgrn, new_state_rwkv)
```

```python
import functools
import math
import jax
import jax.numpy as jnp
from jax import lax
import numpy as np
from jax.experimental import pallas as pl
from jax.experimental.pallas import tpu as pltpu

D_MODEL = 1024
DEPTH = 4
GRID_W = 64
BRANCH_WIDTH = 512
N_BRANCH = 3

A_DK = 128
A_DV = 128
A_HEADS = BRANCH_WIDTH // A_DV

HY_BANDS = 16
HY_EMB = 1 + 2 * HY_BANDS
HY_FFN = 64

C_HEAD_DIM = 64
C_HEADS = BRANCH_WIDTH // C_HEAD_DIM
C_WIDTH = BRANCH_WIDTH
C_DECAY_LORA = 64
C_AAA_LORA = 64
C_GATE_LORA = 128

A_COLS = 3 * A_HEADS * A_DK + A_HEADS * A_DV + BRANCH_WIDTH
B_COLS = 3 * BRANCH_WIDTH
C_COLS = 3 * C_WIDTH + 2 * C_DECAY_LORA + 2 * C_AAA_LORA + C_GATE_LORA
G_COLS = N_BRANCH * D_MODEL

P_HEADS = 8
P_NKEYS = 128
P_EXPERTS = P_NKEYS * P_NKEYS
P_TOPK = 16
P_KEY_DIM = 128

EPS = 1e-6
RWKV_GN_EPS = 64e-5

LANES = 128
ROW_TILE = 256
VMEM_LIMIT = 56 * 1024 * 1024

F32 = jnp.float32
BF16 = jnp.bfloat16


def _cparams(sem):
    return pltpu.CompilerParams(dimension_semantics=sem, vmem_limit_bytes=VMEM_LIMIT)


def _bdot(a, b):
    return jnp.dot(a.astype(BF16), b.astype(BF16), preferred_element_type=F32)


def _split_hi_lo(x):
    hi = x.astype(BF16)
    lo = (x - hi.astype(F32)).astype(BF16)
    return hi, lo


def _seg_of_tile(i, prompt_tiles, tiles_per_sample):
    return jnp.where(i < prompt_tiles, 0, 1 + (i - prompt_tiles) // tiles_per_sample)


def _norm_mod(x, gain, scale, shift):
    return x * lax.rsqrt(jnp.mean(x * x, axis=-1, keepdims=True) + EPS) * gain * (1.0 + scale) + shift


def _inproj_kernel(x_ref, gain_ref, scale_ref, shift_ref, w_ref, o_ref):
    h = _norm_mod(x_ref[...], gain_ref[...], scale_ref[0], shift_ref[0])
    o_ref[...] = jnp.dot(h.astype(BF16), w_ref[...], preferred_element_type=F32)


def norm_mod_matmul(x, gain, scale, shift, w_bf16, prompt_tiles, tiles_per_sample):
    n, d = x.shape
    cols = w_bf16.shape[1]
    seg = lambda i: (_seg_of_tile(i, prompt_tiles, tiles_per_sample), 0, 0)
    return pl.pallas_call(
        _inproj_kernel,
        grid=(n // ROW_TILE,),
        in_specs=[pl.BlockSpec((ROW_TILE, d), lambda i: (i, 0)),
                  pl.BlockSpec((1, d), lambda i: (0, 0)),
                  pl.BlockSpec((1, 1, d), seg),
                  pl.BlockSpec((1, 1, d), seg),
                  pl.BlockSpec((d, cols), lambda i: (0, 0))],
        out_specs=pl.BlockSpec((ROW_TILE, cols), lambda i: (i, 0)),
        out_shape=jax.ShapeDtypeStruct((n, cols), F32),
        compiler_params=_cparams(("parallel",)),
    )(x, gain.reshape(1, d), scale, shift, w_bf16)


HG_CHUNK = 128
HG_LEVELS = 7


def _hgrn_tables():
    c = HG_CHUNK
    out = np.zeros((2, HG_LEVELS + 2, c, c), np.float32)
    t = np.arange(c)
    for d in range(2):
        pos = t if d == 0 else c - 1 - t
        pt, pu = pos[:, None], pos[None, :]
        for lv in range(HG_LEVELS):
            m = c >> (lv + 1)
            same = (pt // m) == (pu // m)
            query = ((pt // m) % 2) == 1
            out[d, lv] = same & np.where(query, pu <= pt, pu > pt)
        out[d, HG_LEVELS] = pu <= pt
        out[d, HG_LEVELS + 1] = pu > pt
    return out.reshape(2, (HG_LEVELS + 2) * c, c)


def _hgrn_kernel(q_ref, fz_ref, v_ref, lb_ref, tab_ref, s0_ref, o_ref, sfin_ref, st_scr, *, n_chunks):
    c = HG_CHUNK
    d = pl.program_id(2)
    ci = pl.program_id(3)

    @pl.when(ci == 0)
    def _():
        st_scr[...] = s0_ref[0, 0, 0].T

    qz = q_ref[...]
    q = qz * jax.nn.sigmoid(qz)
    lb = lb_ref[0]
    f = lb + (1.0 - lb) * jax.nn.sigmoid(fz_ref[...])
    kd = 1.0 - f
    logf = jnp.log(f)
    v = v_ref[...]

    lhi, llo = _split_hi_lo(logf)
    tab = tab_ref[0]
    args = (jnp.dot(tab, lhi, preferred_element_type=F32)
            + jnp.dot(tab, llo, preferred_element_type=F32))
    e_all = jnp.exp(args)

    row = lax.broadcasted_iota(jnp.int32, (c, LANES), 0)
    pos_r = row + d * (c - 1 - 2 * row)
    ti = lax.broadcasted_iota(jnp.int32, (c, c), 0)
    si = lax.broadcasted_iota(jnp.int32, (c, c), 1)
    pos_t = ti + d * (c - 1 - 2 * ti)
    pos_s = si + d * (c - 1 - 2 * si)

    attn = jnp.where(ti == si, jnp.sum(q * kd, axis=-1, keepdims=True), 0.0)
    for lv in range(HG_LEVELS):
        sh = HG_LEVELS - 1 - lv
        e = e_all[lv * c:(lv + 1) * c]
        is_q = ((pos_r >> sh) & 1) == 1
        qs = jnp.where(is_q, q * e, 0.0).astype(BF16)
        ks = jnp.where(is_q, 0.0, kd * e).astype(BF16)
        sc = lax.dot_general(qs, ks, (((1,), (1,)), ((), ())), preferred_element_type=F32)
        attn = attn + jnp.where((pos_t >> (sh + 1)) == (pos_s >> (sh + 1)), sc, 0.0)

    e_cum = e_all[HG_LEVELS * c:(HG_LEVELS + 1) * c]
    e_rev = e_all[(HG_LEVELS + 1) * c:]
    st = st_scr[...]
    o = _bdot(attn, v) + lax.dot_general((q * e_cum).astype(BF16), st.astype(BF16),
                                         (((1,), (1,)), ((), ())), preferred_element_type=F32)
    o_ref[0, 0] = o
    dec = jnp.exp(jnp.sum(logf, axis=0, keepdims=True))
    st_new = st * dec + _bdot(v.T, kd * e_rev)
    st_scr[...] = st_new

    @pl.when(ci == n_chunks - 1)
    def _():
        sfin_ref[0, 0, 0] = st_new.T


def hgrn2_scan(pa, lb, s0, row_off, batch, seqlen):
    c = HG_CHUNK
    n_chunks = seqlen // c
    base = row_off // c
    tab = jnp.asarray(_hgrn_tables(), BF16)

    def rows(b, ci, d):
        return base + b * n_chunks + ci + d * (n_chunks - 1 - 2 * ci)

    def orow(b, ci, d):
        return b * n_chunks + ci + d * (n_chunks - 1 - 2 * ci)

    return pl.pallas_call(
        functools.partial(_hgrn_kernel, n_chunks=n_chunks),
        grid=(batch, A_HEADS, 2, n_chunks),
        in_specs=[pl.BlockSpec((c, LANES), lambda b, h, d, ci: (rows(b, ci, d), h)),
                  pl.BlockSpec((c, LANES), lambda b, h, d, ci: (rows(b, ci, d), A_HEADS + A_HEADS * d + h)),
                  pl.BlockSpec((c, LANES), lambda b, h, d, ci: (rows(b, ci, d), 3 * A_HEADS + h)),
                  pl.BlockSpec((1, 1, LANES), lambda b, h, d, ci: (d, 0, h)),
                  pl.BlockSpec((1, (HG_LEVELS + 2) * c, c), lambda b, h, d, ci: (d, 0, 0)),
                  pl.BlockSpec((1, 1, 1, A_DK, A_DV), lambda b, h, d, ci: (b, d, h, 0, 0))],
        out_specs=[pl.BlockSpec((1, 1, c, LANES), lambda b, h, d, ci: (d, 0, orow(b, ci, d), h)),
                   pl.BlockSpec((1, 1, 1, A_DK, A_DV), lambda b, h, d, ci: (b, d, h, 0, 0))],
        out_shape=[jax.ShapeDtypeStruct((2, 1, batch * seqlen, A_HEADS * A_DV), F32),
                   jax.ShapeDtypeStruct((batch, 2, A_HEADS, A_DK, A_DV), F32)],
        scratch_shapes=[pltpu.VMEM((A_DV, A_DK), F32)],
        compiler_params=_cparams(("parallel", "parallel", "parallel", "arbitrary")),
    )(pa, pa, pa, lb, tab, s0)


def _dft_table_kernel(fc_ref, fs_ref, gc_ref, gs_ref, *, length, tile):
    n2 = 2 * length
    a = lax.broadcasted_iota(jnp.int32, (tile, length), 0) + pl.program_id(0) * tile
    b = lax.broadcasted_iota(jnp.int32, (tile, length), 1)
    ang = ((a * b) & (n2 - 1)).astype(F32) * (2.0 * math.pi / n2)
    co, si = jnp.cos(ang), jnp.sin(ang)
    alt_b = (1 - 2 * (b & 1)).astype(F32)
    alt_a = (1 - 2 * (a & 1)).astype(F32)
    fc_ref[...] = co.astype(BF16)
    fs_ref[...] = jnp.where(a == 0, alt_b, -si).astype(BF16)
    wk = jnp.where(b == 0, 1.0 / n2, 2.0 / n2)
    gc_ref[...] = (wk * co).astype(BF16)
    gs_ref[...] = jnp.where(b == 0, alt_a * (1.0 / n2), -wk * si).astype(BF16)


def dft_tables(length):
    tile = min(length, 256)
    spec = pl.BlockSpec((tile, length), lambda i: (i, 0))
    shp = jax.ShapeDtypeStruct((length, length), BF16)
    return pl.pallas_call(
        functools.partial(_dft_table_kernel, length=length, tile=tile),
        grid=(length // tile,),
        out_specs=[spec] * 4, out_shape=[shp] * 4,
        compiler_params=_cparams(("parallel",)),
    )()


def _dft_fwd_kernel(fc_ref, fs_ref, z_ref, zr_ref, zi_ref):
    z = z_ref[0].astype(BF16)
    zr_ref[0] = jnp.dot(fc_ref[...], z, preferred_element_type=F32)
    zi_ref[0] = jnp.dot(fs_ref[...], z, preferred_element_type=F32)


def dft_forward(fc, fs, z):
    batch, length, ch = z.shape
    tile = min(length, 512)
    return pl.pallas_call(
        _dft_fwd_kernel,
        grid=(length // tile, batch),
        in_specs=[pl.BlockSpec((tile, length), lambda i, b: (i, 0)),
                  pl.BlockSpec((tile, length), lambda i, b: (i, 0)),
                  pl.BlockSpec((1, length, ch), lambda i, b: (b, 0, 0))],
        out_specs=[pl.BlockSpec((1, tile, ch), lambda i, b: (b, i, 0))] * 2,
        out_shape=[jax.ShapeDtypeStruct((batch, length, ch), F32)] * 2,
        compiler_params=_cparams(("parallel", "arbitrary")),
    )(fc, fs, z)


def _spec_mul_kernel(zr_ref, zi_ref, kr_ref, ki_ref, yr_ref, yi_ref, *, tile):
    row = lax.broadcasted_iota(jnp.int32, (tile, 1), 0) + pl.program_id(1) * tile
    sgn = (1 - 2 * (row & 1)).astype(F32)
    kr = kr_ref[0] + sgn * kr_ref[1]
    ki = ki_ref[0] + sgn * ki_ref[1]
    zr, zi = zr_ref[0], zi_ref[0]
    packed = row == 0
    yr_ref[0] = jnp.where(packed, zr * kr, zr * kr - zi * ki).astype(BF16)
    yi_ref[0] = jnp.where(packed, zi * ki, zr * ki + zi * kr).astype(BF16)


def spectrum_multiply(zr, zi, kr, ki):
    batch, length, ch = zr.shape
    tile = min(length, 512)
    zspec = pl.BlockSpec((1, tile, ch), lambda b, i: (b, i, 0))
    kspec = pl.BlockSpec((2, tile, ch), lambda b, i: (0, i, 0))
    return pl.pallas_call(
        functools.partial(_spec_mul_kernel, tile=tile),
        grid=(batch, length // tile),
        in_specs=[zspec, zspec, kspec, kspec],
        out_specs=[zspec, zspec],
        out_shape=[jax.ShapeDtypeStruct((batch, length, ch), BF16)] * 2,
        compiler_params=_cparams(("parallel", "parallel")),
    )(zr, zi, kr, ki)


def _dft_inv_kernel(gc_ref, gs_ref, yr_ref, yi_ref, x0_ref, z_ref, bias_ref, o_ref):
    y = (jnp.dot(gc_ref[...], yr_ref[0], preferred_element_type=F32)
         + jnp.dot(gs_ref[...], yi_ref[0], preferred_element_type=F32))
    o_ref[0] = x0_ref[0] * (y + z_ref[0] * bias_ref[...])


def dft_inverse_gate(gc, gs, yr, yi, x0, z, bias):
    batch, length, ch = yr.shape
    tile = min(length, 512)
    tspec = pl.BlockSpec((1, tile, ch), lambda i, b: (b, i, 0))
    return pl.pallas_call(
        _dft_inv_kernel,
        grid=(length // tile, batch),
        in_specs=[pl.BlockSpec((tile, length), lambda i, b: (i, 0)),
                  pl.BlockSpec((tile, length), lambda i, b: (i, 0)),
                  pl.BlockSpec((1, length, ch), lambda i, b: (b, 0, 0)),
                  pl.BlockSpec((1, length, ch), lambda i, b: (b, 0, 0)),
                  tspec, tspec,
                  pl.BlockSpec((1, ch), lambda i, b: (0, 0))],
        out_specs=tspec,
        out_shape=jax.ShapeDtypeStruct((batch, length, ch), F32),
        compiler_params=_cparams(("parallel", "arbitrary")),
    )(gc, gs, yr, yi, x0, z, bias.reshape(1, ch))


def shift_prev(x):
    return jnp.pad(x, ((0, 0), (1, 0), (0, 0)))[:, :-1]


def shift_next(x):
    return jnp.pad(x, ((0, 0), (0, 1), (0, 0)))[:, 1:]


def hyena_filter(length, lp):
    hp = lax.Precision.HIGHEST
    t = jnp.linspace(0.0, 1.0, length, dtype=F32)[:, None]
    bands = jnp.linspace(1e-4, HY_BANDS - 1, HY_BANDS, dtype=F32)[None, :]
    ang = (2 * math.pi / length) * jnp.arange(length, dtype=F32)[:, None] * bands
    z = jnp.concatenate([t, jnp.cos(ang), -jnp.sin(ang)], axis=-1)
    hf = jnp.sin(lp['hy_freq1'] * (jnp.dot(z, lp['hy_w1'], precision=hp) + lp['hy_b1']))
    hf = jnp.sin(lp['hy_freq2'] * (jnp.dot(hf, lp['hy_w2'], precision=hp) + lp['hy_b2']))
    hf = jnp.dot(hf, lp['hy_w3'], precision=hp).reshape(length, 2, BRANCH_WIDTH)
    hf = hf * jnp.exp(-t * jnp.abs(lp['hy_delta']))[:, None, :]
    lo = hf[:, 0]
    hi = jnp.concatenate([jnp.zeros((1, BRANCH_WIDTH), F32), hf[:0:-1, 1]], axis=0)
    norm = jnp.sum(jnp.abs(lo), axis=0, keepdims=True) + jnp.sum(jnp.abs(hi), axis=0, keepdims=True)
    return jnp.stack([lo, hi]) / norm


def hyena_branch(pb, lp, tables):
    batch, length, _ = pb.shape
    fc, fs, gc, gs = tables
    cw = lp['hy_conv_w']
    u = shift_prev(pb) * cw[0] + pb * cw[1] + shift_next(pb) * cw[2] + lp['hy_conv_b']
    x0, x1, v = jnp.split(u, 3, axis=-1)
    z = v * x1
    kr, ki = dft_forward(fc, fs, hyena_filter(length, lp))
    zr, zi = dft_forward(fc, fs, z)
    yr, yi = spectrum_multiply(zr, zi, kr, ki)
    return dft_inverse_gate(gc, gs, yr, yi, x0, z, lp['hy_bias'])


RW_CHAINS = 64
RW_VROWS = C_HEAD_DIM // 2
RW_TCHUNK = 32


def _rwkv_scan_kernel(r_ref, w_ref, k_ref, a_ref, b_ref, v_ref, s0_ref, y_ref, sfin_ref, s_scr, *, n_tc):
    tc = pl.program_id(1)

    @pl.when(tc == 0)
    def _():
        s_scr[...] = s0_ref[0]

    nk = C_HEAD_DIM

    def step(t, carry):
        parts = [jnp.zeros((RW_VROWS, LANES), F32) for _ in range(4)]
        for kk in range(nk):
            parts[kk % 4] = parts[kk % 4] + s_scr[kk] * a_ref[0, t, pl.ds(kk, 1), :]
        sa = (parts[0] + parts[1]) + (parts[2] + parts[3])
        v_t = v_ref[0, t]
        ys = [jnp.zeros((RW_VROWS, LANES), F32) for _ in range(4)]
        for kk in range(nk):
            s_new = (s_scr[kk] * w_ref[0, t, pl.ds(kk, 1), :] + sa * b_ref[0, t, pl.ds(kk, 1), :]
                     + v_t * k_ref[0, t, pl.ds(kk, 1), :])
            s_scr[kk] = s_new
            ys[kk % 4] = ys[kk % 4] + s_new * r_ref[0, t, pl.ds(kk, 1), :]
        y_ref[0, t] = (ys[0] + ys[1]) + (ys[2] + ys[3])
        return carry

    lax.fori_loop(0, RW_TCHUNK, step, 0)

    @pl.when(tc == n_tc - 1)
    def _():
        sfin_ref[0] = s_scr[...]


def rwkv7_scan_lanes(r, w, k, a, b, v, s0):
    g, t = r.shape[0], r.shape[1]
    n_tc = t // RW_TCHUNK
    kspec = pl.BlockSpec((1, RW_TCHUNK, C_HEAD_DIM, LANES), lambda gi, ti: (gi, ti, 0, 0))
    vspec = pl.BlockSpec((1, RW_TCHUNK, RW_VROWS, LANES), lambda gi, ti: (gi, ti, 0, 0))
    sspec = pl.BlockSpec((1, C_HEAD_DIM, RW_VROWS, LANES), lambda gi, ti: (gi, 0, 0, 0))
    return pl.pallas_call(
        functools.partial(_rwkv_scan_kernel, n_tc=n_tc),
        grid=(g, n_tc),
        in_specs=[kspec] * 5 + [vspec, sspec],
        out_specs=[vspec, sspec],
        out_shape=[jax.ShapeDtypeStruct((g, t, RW_VROWS, LANES), F32),
                   jax.ShapeDtypeStruct((g, C_HEAD_DIM, RW_VROWS, LANES), F32)],
        scratch_shapes=[pltpu.VMEM((C_HEAD_DIM, RW_VROWS, LANES), F32)],
        compiler_params=_cparams(("parallel", "arbitrary")),
    )(r, w, k, a, b, v, s0)


def _to_key_lanes(x):
    _, bsz, t, _, _ = x.shape
    chains = 2 * bsz * C_HEADS
    g = chains // RW_CHAINS
    x = jnp.transpose(x, (2, 4, 0, 1, 3)).reshape(t, C_HEAD_DIM, g, RW_CHAINS)
    x = jnp.transpose(x, (2, 0, 1, 3))
    return jnp.concatenate([x, x], axis=-1)


def _to_value_lanes(x):
    _, bsz, t, _, _ = x.shape
    g = 2 * bsz * C_HEADS // RW_CHAINS
    x = jnp.transpose(x, (2, 4, 0, 1, 3)).reshape(t, 2, RW_VROWS, g, RW_CHAINS)
    return jnp.transpose(x, (3, 0, 2, 1, 4)).reshape(g, t, RW_VROWS, LANES)


def _from_value_lanes(y, bsz):
    g, t = y.shape[0], y.shape[1]
    y = y.reshape(g, t, RW_VROWS, 2, RW_CHAINS)
    y = jnp.transpose(y, (1, 3, 2, 0, 4)).reshape(t, C_HEAD_DIM, 2, bsz, C_HEADS)
    return jnp.transpose(y, (2, 3, 0, 4, 1))


def _state_to_lanes(s):
    bsz = s.shape[0]
    g = 2 * bsz * C_HEADS // RW_CHAINS
    s = jnp.transpose(s, (4, 3, 1, 0, 2)).reshape(C_HEAD_DIM, 2, RW_VROWS, g, RW_CHAINS)
    return jnp.transpose(s, (3, 0, 2, 1, 4)).reshape(g, C_HEAD_DIM, RW_VROWS, LANES)


def _state_from_lanes(s, bsz):
    g = s.shape[0]
    s = s.reshape(g, C_HEAD_DIM, RW_VROWS, 2, RW_CHAINS)
    s = jnp.transpose(s, (1, 3, 2, 0, 4)).reshape(C_HEAD_DIM, C_HEAD_DIM, 2, bsz, C_HEADS)
    return jnp.transpose(s, (3, 2, 4, 1, 0))


def neighbour_mean(x, latent):
    if not latent:
        return 0.5 * (shift_prev(x) + shift_next(x))
    B, T, C = x.shape
    rows = T // GRID_W
    gp = jnp.pad(x.reshape(B, rows, GRID_W, C), ((0, 0), (1, 1), (1, 1), (0, 0)))
    s = gp[:, :-2, 1:-1] + gp[:, 2:, 1:-1] + gp[:, 1:-1, :-2] + gp[:, 1:-1, 2:]
    return (0.25 * s).reshape(B, T, C)


def rwkv7_branch(pc, S0, lp, latent):
    B, T, _ = pc.shape
    pc = pc + (neighbour_mean(pc, latent) - pc) * lp['rw_mu']
    cut = [C_WIDTH, 2 * C_WIDTH, 3 * C_WIDTH, 3 * C_WIDTH + 2 * C_DECAY_LORA,
           3 * C_WIDTH + 2 * C_DECAY_LORA + 2 * C_AAA_LORA]
    r, k, v, wd, ad, gd = jnp.split(pc, cut, axis=-1)
    heads = lambda z: z.reshape(B, T, C_HEADS, C_HEAD_DIM)
    kk = heads(k * lp['rw_kk'])
    kk = kk * lax.rsqrt(jnp.sum(kk * kk, axis=-1, keepdims=True) + 1e-12)
    g = jax.nn.sigmoid(gd) @ lp['rw_g2']
    wd = wd.reshape(B, T, 2, C_DECAY_LORA)
    ad = ad.reshape(B, T, 2, C_AAA_LORA)
    seqs = []
    for d in range(2):
        w = -jax.nn.softplus(-(lp['rw_w0'][d] + jnp.tanh(wd[:, :, d]) @ lp['rw_w2'][d])) - 0.5
        decay = jnp.exp(-jnp.exp(w))
        a = jax.nn.sigmoid(lp['rw_a0'][d] + ad[:, :, d] @ lp['rw_a2'][d])
        kd = k * (1 + (a - 1) * lp['rw_ka'])
        seq = (heads(r), heads(decay), heads(kd), -kk, kk * heads(a), heads(v))
        if d == 1:
            seq = tuple(z[:, ::-1] for z in seq)
        seqs.append(seq)
    stacked = [jnp.stack([seqs[0][i], seqs[1][i]]) for i in range(6)]
    ylanes, sfin = rwkv7_scan_lanes(*[_to_key_lanes(z) for z in stacked[:5]],
                                    _to_value_lanes(stacked[5]), _state_to_lanes(S0))
    ys = _from_value_lanes(ylanes, B)
    y = ys[0] + ys[1][:, ::-1]
    mu = jnp.mean(y, axis=-1, keepdims=True)
    var = jnp.mean(jnp.square(y - mu), axis=-1, keepdims=True)
    y = ((y - mu) * lax.rsqrt(var + RWKV_GN_EPS)).reshape(B, T, C_WIDTH) * lp['rw_ln_g'] + lp['rw_ln_b']
    bonus = jnp.sum(heads(r) * heads(k) * lp['rw_rk'], axis=-1, keepdims=True) * heads(v)
    return (y + bonus.reshape(B, T, C_WIDTH)) * g, _state_from_lanes(sfin, B)


def _merge_kernel(x_ref, oa_ref, ga_ref, gn_ref, yb_ref, yc_ref, pg_ref, gate_ref, wb_ref, wo_ref, o_ref):
    oa = oa_ref[0, 0] + oa_ref[1, 0]
    gz = ga_ref[...]
    gn = gn_ref[...]
    parts = []
    for h in range(A_HEADS):
        oh = oa[:, h * A_DV:(h + 1) * A_DV]
        parts.append(oh * lax.rsqrt(jnp.mean(oh * oh, axis=-1, keepdims=True) + EPS) * gn)
    ya = jnp.concatenate(parts, axis=-1) * (gz * jax.nn.sigmoid(gz))
    pg = pg_ref[...]
    d = D_MODEL
    merged = (jax.nn.sigmoid(pg[:, :d]) * _bdot(ya, wb_ref[0])
              + jax.nn.sigmoid(pg[:, d:2 * d]) * _bdot(yb_ref[...], wb_ref[1])
              + jax.nn.sigmoid(pg[:, 2 * d:]) * _bdot(yc_ref[...], wb_ref[2]))
    o_ref[...] = x_ref[...] + gate_ref[0] * _bdot(merged, wo_ref[...])


def merge_out(x, o_hgrn, pa, hgrn_norm, yb, yc, pg, gate, wb_bf16, wo_bf16, prompt_tiles, tiles_per_sample):
    n, d = x.shape
    bw = BRANCH_WIDTH
    row = lambda i: (i, 0)
    seg = lambda i: (_seg_of_tile(i, prompt_tiles, tiles_per_sample), 0, 0)
    return pl.pallas_call(
        _merge_kernel,
        grid=(n // ROW_TILE,),
        in_specs=[pl.BlockSpec((ROW_TILE, d), row),
                  pl.BlockSpec((2, 1, ROW_TILE, bw), lambda i: (0, 0, i, 0)),
                  pl.BlockSpec((ROW_TILE, bw), lambda i: (i, (A_COLS - bw) // bw)),
                  pl.BlockSpec((1, A_DV), lambda i: (0, 0)),
                  pl.BlockSpec((ROW_TILE, bw), row),
                  pl.BlockSpec((ROW_TILE, bw), row),
                  pl.BlockSpec((ROW_TILE, G_COLS), row),
                  pl.BlockSpec((1, 1, d), seg),
                  pl.BlockSpec((N_BRANCH, bw, d), lambda i: (0, 0, 0)),
                  pl.BlockSpec((d, d), lambda i: (0, 0))],
        out_specs=pl.BlockSpec((ROW_TILE, d), row),
        out_shape=jax.ShapeDtypeStruct((n, d), F32),
        compiler_params=_cparams(("parallel",)),
    )(x, o_hgrn, pa, hgrn_norm.reshape(1, A_DV), yb, yc, pg, gate, wb_bf16, wo_bf16)


PEER_SEL_TILE = 256
PEER_TOK_TILE = 512
PEER_EXP_TILE = 1024
PEER_RANKS = P_TOPK + 1


def _nt_x3_kernel(a_ref, b_ref, hi_ref, lo_ref):
    ahi, alo = _split_hi_lo(a_ref[...])
    bhi, blo = _split_hi_lo(b_ref[...])
    nt = lambda p, q: lax.dot_general(p, q, (((1,), (1,)), ((), ())), preferred_element_type=F32)
    m = nt(ahi, bhi) + (nt(ahi, blo) + nt(alo, bhi))
    hi, lo = _split_hi_lo(m)
    hi_ref[...] = hi
    lo_ref[...] = lo


def peer_score_matrix(p_wq, p_key1, p_key2):
    d = p_wq.shape[0]
    half = P_KEY_DIM // 2
    eye = jnp.eye(P_HEADS, dtype=F32)
    k1 = jnp.einsum('nd,hg->hngd', p_key1, eye)
    k2 = jnp.einsum('nd,hg->hngd', p_key2, eye)
    zeros = jnp.zeros_like(k1)
    kbig = jnp.stack([jnp.stack([k1, zeros], axis=3), jnp.stack([zeros, k2], axis=3)])
    kbig = kbig.reshape(2 * P_HEADS * P_NKEYS, P_HEADS * P_KEY_DIM)
    rows = kbig.shape[0]
    tile = 256
    return pl.pallas_call(
        _nt_x3_kernel,
        grid=(rows // tile,),
        in_specs=[pl.BlockSpec((tile, P_HEADS * P_KEY_DIM), lambda i: (i, 0)),
                  pl.BlockSpec((d, P_HEADS * P_KEY_DIM), lambda i: (0, 0))],
        out_specs=[pl.BlockSpec((tile, d), lambda i: (i, 0))] * 2,
        out_shape=[jax.ShapeDtypeStruct((rows, d), BF16)] * 2,
        compiler_params=_cparams(("parallel",)),
    )(kbig, p_wq)


def _peer_scores_kernel(x_ref, gain_ref, scale_ref, shift_ref, mhi_ref, mlo_ref, ht_ref, s_ref):
    h = _norm_mod(x_ref[...], gain_ref[...], scale_ref[0], shift_ref[0])
    ht = h.T
    hhi, hlo = _split_hi_lo(ht)
    ht_ref[...] = hhi
    mhi = mhi_ref[...]
    s_ref[...] = (jnp.dot(mhi, hhi, preferred_element_type=F32)
                  + (jnp.dot(mhi, hlo, preferred_element_type=F32)
                     + jnp.dot(mlo_ref[...], hhi, preferred_element_type=F32)))


def peer_scores(x, gain, scale, shift, mhi, mlo, prompt_tiles, tiles_per_sample):
    n, d = x.shape
    rows = mhi.shape[0]
    seg = lambda i: (_seg_of_tile(i, prompt_tiles, tiles_per_sample), 0, 0)
    return pl.pallas_call(
        _peer_scores_kernel,
        grid=(n // ROW_TILE,),
        in_specs=[pl.BlockSpec((ROW_TILE, d), lambda i: (i, 0)),
                  pl.BlockSpec((1, d), lambda i: (0, 0)),
                  pl.BlockSpec((1, 1, d), seg),
                  pl.BlockSpec((1, 1, d), seg),
                  pl.BlockSpec((rows, d), lambda i: (0, 0)),
                  pl.BlockSpec((rows, d), lambda i: (0, 0))],
        out_specs=[pl.BlockSpec((d, ROW_TILE), lambda i: (0, i)),
                   pl.BlockSpec((rows, ROW_TILE), lambda i: (0, i))],
        out_shape=[jax.ShapeDtypeStruct((d, n), BF16),
                   jax.ShapeDtypeStruct((rows, n), F32)],
        compiler_params=_cparams(("parallel",)),
    )(x, gain.reshape(1, d), scale, shift, mhi, mlo)


def _peer_select_kernel(s_ref, thr_ref, e1_ref, e2_ref, v1_scr, v2_scr, st_scr):
    neg = -jnp.inf
    nr = PEER_RANKS
    for h in range(P_HEADS):
        for half, scr in ((0, v1_scr), (1, v2_scr)):
            s = s_ref[half, h]
            m = jnp.max(s, axis=0, keepdims=True)
            scr[0, pl.ds(h, 1), :] = m
            for rnk in range(1, nr):
                m = jnp.max(jnp.where(s < m, s, neg), axis=0, keepdims=True)
                scr[rnk, pl.ds(h, 1), :] = m
    cands = []
    for a in range(nr):
        for b in range(nr):
            if (a + 1) * (b + 1) <= nr:
                cands.append(v1_scr[a] + v2_scr[b])
    top = cands[0]
    m = top
    for rnk in range(1, nr):
        prev = m
        m = None
        for cnd in cands:
            x = jnp.where(cnd < prev, cnd, neg)
            m = x if m is None else jnp.maximum(m, x)
        if rnk == nr - 2:
            t16 = m
    t17 = m
    tau = 0.5 * (t16 + t17)
    zsum = jnp.zeros_like(top)
    for cnd in cands:
        zsum = zsum + jnp.where(cnd > tau, jnp.exp(cnd - top), 0.0)
    st_scr[0] = tau
    st_scr[1] = 1.0 / zsum
    for h in range(P_HEADS):
        s1 = s_ref[0, h]
        s2 = s_ref[1, h]
        thr_ref[h] = st_scr[0, pl.ds(h, 1), :] - s1
        e1_ref[h] = jnp.exp(s1 - v1_scr[0, pl.ds(h, 1), :])
        e2_ref[h] = jnp.exp(s2 - v2_scr[0, pl.ds(h, 1), :]) * st_scr[1, pl.ds(h, 1), :]


def peer_select(s4):
    n = s4.shape[-1]
    tl = PEER_SEL_TILE
    ospec = pl.BlockSpec((P_HEADS, P_NKEYS, tl), lambda i: (0, 0, i))
    oshape = jax.ShapeDtypeStruct((P_HEADS, P_NKEYS, n), F32)
    return pl.pallas_call(
        _peer_select_kernel,
        grid=(n // tl,),
        in_specs=[pl.BlockSpec((2, P_HEADS, P_NKEYS, tl), lambda i: (0, 0, 0, i))],
        out_specs=[ospec] * 3, out_shape=[oshape] * 3,
        scratch_shapes=[pltpu.VMEM((PEER_RANKS, P_HEADS, tl), F32),
                        pltpu.VMEM((PEER_RANKS, P_HEADS, tl), F32),
                        pltpu.VMEM((2, P_HEADS, tl), F32)],
        compiler_params=_cparams(("parallel",)),
    )(s4)


def _peer_expert_kernel(ht_ref, u_ref, vt_ref, thr_ref, e1_ref, s2_ref, e2_ref, x_ref, gate_ref,
                        o_ref, acc_scr, w_scr, *, n_eb):
    j = pl.program_id(1)

    @pl.when(j == 0)
    def _():
        acc_scr[...] = jnp.zeros_like(acc_scr)

    at = jnp.dot(u_ref[...], ht_ref[...], preferred_element_type=F32)
    n_i = PEER_EXP_TILE // P_NKEYS
    sub = 8
    for il in range(n_i):
        for jt in range(P_NKEYS // sub):
            w = None
            for h in range(P_HEADS):
                sel = jnp.where(s2_ref[0, h, pl.ds(jt * sub, sub), :] >= thr_ref[h, pl.ds(il, 1), :],
                                e2_ref[h, pl.ds(jt * sub, sub), :], 0.0) * e1_ref[h, pl.ds(il, 1), :]
                w = sel if w is None else w + sel
            w_scr[pl.ds(il * P_NKEYS + jt * sub, sub), :] = w
    p = (w_scr[...] * jax.nn.gelu(at)).astype(BF16)
    acc_scr[...] += jnp.dot(vt_ref[...], p, preferred_element_type=F32)

    @pl.when(j == n_eb - 1)
    def _():
        o_ref[...] = x_ref[...] + gate_ref[0] * acc_scr[...].T


def peer_experts(ht, u_bf16, vt_bf16, thr, e1, s4, e2, x, gate, prompt_tiles, tiles_per_sample):
    n, d = x.shape
    tt, eb = PEER_TOK_TILE, PEER_EXP_TILE
    n_eb = P_EXPERTS // eb
    n_i = eb // P_NKEYS
    scale = tt // ROW_TILE
    seg = lambda i, j: (_seg_of_tile(i * scale, prompt_tiles, tiles_per_sample), 0, 0)
    full = pl.BlockSpec((P_HEADS, P_NKEYS, tt), lambda i, j: (0, 0, i))
    part = pl.BlockSpec((P_HEADS, n_i, tt), lambda i, j: (0, j, i))
    return pl.pallas_call(
        functools.partial(_peer_expert_kernel, n_eb=n_eb),
        grid=(n // tt, n_eb),
        in_specs=[pl.BlockSpec((d, tt), lambda i, j: (0, i)),
                  pl.BlockSpec((eb, d), lambda i, j: (j, 0)),
                  pl.BlockSpec((d, eb), lambda i, j: (0, j)),
                  part, part,
                  pl.BlockSpec((1, P_HEADS, P_NKEYS, tt), lambda i, j: (1, 0, 0, i)),
                  full,
                  pl.BlockSpec((tt, d), lambda i, j: (i, 0)),
                  pl.BlockSpec((1, 1, d), seg)],
        out_specs=pl.BlockSpec((tt, d), lambda i, j: (i, 0)),
        out_shape=jax.ShapeDtypeStruct((n, d), F32),
        scratch_shapes=[pltpu.VMEM((d, tt), F32), pltpu.VMEM((eb, tt), F32)],
        compiler_params=_cparams(("parallel", "arbitrary")),
    )(ht, u_bf16, vt_bf16, thr, e1, s4, e2, x, gate)


def _final_norm_kernel(x_ref, g_ref, o_ref):
    x = x_ref[...]
    o_ref[...] = x * lax.rsqrt(jnp.mean(x * x, axis=-1, keepdims=True) + EPS) * g_ref[...]


def final_norm(x, gain):
    n, d = x.shape
    return pl.pallas_call(
        _final_norm_kernel,
        grid=(n // ROW_TILE,),
        in_specs=[pl.BlockSpec((ROW_TILE, d), lambda i: (i, 0)), pl.BlockSpec((1, d), lambda i: (0, 0))],
        out_specs=pl.BlockSpec((ROW_TILE, d), lambda i: (i, 0)),
        out_shape=jax.ShapeDtypeStruct((n, d), F32),
        compiler_params=_cparams(("parallel",)),
    )(x, gain.reshape(1, d))


def trunk(x_prompt, x_sample, state_hgrn, state_rwkv, c, c_ctx, params):
    bp, tp, d = x_prompt.shape
    bs, ts, _ = x_sample.shape
    n_p, n_s = bp * tp, bs * ts
    prompt_tiles, tiles_per_sample = n_p // ROW_TILE, ts // ROW_TILE
    tiling = (prompt_tiles, tiles_per_sample)
    x = jnp.concatenate([x_prompt.reshape(n_p, d), x_sample.reshape(n_s, d)], axis=0)
    cvec = jnp.concatenate([c_ctx[None], c], axis=0)

    lb_p = jax.nn.softmax(params['hgrn_lb_logits'], axis=1)
    lb = jnp.cumsum(lb_p, axis=1)
    lb = lb - lb[:, :1]
    tables = {tp: dft_tables(tp), ts: dft_tables(ts)}
    zero_a = jnp.zeros((bp, 2, A_HEADS, A_DK, A_DV), F32)
    zero_c = jnp.zeros((bp, 2, C_HEADS, C_HEAD_DIM, C_HEAD_DIM), F32)

    new_a, new_c = [], []
    for l in range(DEPTH):
        lp = {k: v[l] for k, v in params.items() if k not in ('hgrn_lb_logits', 'final_norm')}
        mod = jnp.dot(jax.nn.silu(cvec), lp['w_ada'], precision=lax.Precision.HIGHEST) + lp['b_ada']
        sh1, sc1, g1, sh2, sc2, g2 = [m[:, None, :] for m in jnp.split(mod, 6, axis=-1)]

        w_in = lp['w_in'].astype(BF16)
        splits = np.cumsum([0, A_COLS, B_COLS, C_COLS, G_COLS])
        pa, pb, pc, pg = [norm_mod_matmul(x, lp['norm1'], sc1, sh1, w_in[:, splits[i]:splits[i + 1]], *tiling)
                          for i in range(4)]

        lb_l = lb[:, l][:, None, :]
        oa_p, sa_p = hgrn2_scan(pa, lb_l, zero_a, 0, bp, tp)
        oa_s, _ = hgrn2_scan(pa, lb_l, state_hgrn[:, l], n_p, bs, ts)
        o_hgrn = jnp.concatenate([oa_p, oa_s], axis=2)

        yb = jnp.concatenate([hyena_branch(pb[:n_p].reshape(bp, tp, B_COLS), lp, tables[tp]).reshape(n_p, -1),
                              hyena_branch(pb[n_p:].reshape(bs, ts, B_COLS), lp, tables[ts]).reshape(n_s, -1)])

        yc_p, sc_p = rwkv7_branch(pc[:n_p].reshape(bp, tp, C_COLS), zero_c, lp, False)
        yc_s, _ = rwkv7_branch(pc[n_p:].reshape(bs, ts, C_COLS), state_rwkv[:, l], lp, True)
        yc = jnp.concatenate([yc_p.reshape(n_p, -1), yc_s.reshape(n_s, -1)])

        x = merge_out(x, o_hgrn, pa, lp['hgrn_norm'], yb, yc, pg, g1,
                      lp['w_branch'].astype(BF16), lp['w_out'].astype(BF16), *tiling)

        mhi, mlo = peer_score_matrix(lp['p_wq'], lp['p_key1'], lp['p_key2'])
        ht, st = peer_scores(x, lp['norm2'], sc2, sh2, mhi, mlo, *tiling)
        s4 = st.reshape(2, P_HEADS, P_NKEYS, n_p + n_s)
        thr, e1, e2 = peer_select(s4)
        x = peer_experts(ht, lp['p_u'].astype(BF16), lp['p_v'].T.astype(BF16), thr, e1, s4, e2, x, g2, *tiling)

        new_a.append(sa_p)
        new_c.append(sc_p)

    y = final_norm(x, params['final_norm'])
    return (y[:n_p].reshape(bp, tp, d), y[n_p:].reshape(bs, ts, d),
            jnp.stack(new_a, axis=1), jnp.stack(new_c, axis=1))


def kernel(x_prompt, x_sample, state_hgrn, state_rwkv, c, c_ctx, w_ada, b_ada, norm1, norm2, w_in,
           hgrn_lb_logits, hgrn_norm, hy_conv_w, hy_conv_b, hy_w1, hy_b1, hy_freq1, hy_w2, hy_b2,
           hy_freq2, hy_w3, hy_delta, hy_bias, rw_mu, rw_w0, rw_w2, rw_a0, rw_a2, rw_g2, rw_kk, rw_ka,
           rw_rk, rw_ln_g, rw_ln_b, w_branch, w_out, p_wq, p_key1, p_key2, p_u, p_v, final_norm):
    params = dict(w_ada=w_ada, b_ada=b_ada, norm1=norm1, norm2=norm2, w_in=w_in,
                  hgrn_lb_logits=hgrn_lb_logits, hgrn_norm=hgrn_norm, hy_conv_w=hy_conv_w,
                  hy_conv_b=hy_conv_b, hy_w1=hy_w1, hy_b1=hy_b1, hy_freq1=hy_freq1, hy_w2=hy_w2,
                  hy_b2=hy_b2, hy_freq2=hy_freq2, hy_w3=hy_w3, hy_delta=hy_delta, hy_bias=hy_bias,
                  rw_mu=rw_mu, rw_w0=rw_w0, rw_w2=rw_w2, rw_a0=rw_a0, rw_a2=rw_a2, rw_g2=rw_g2,
                  rw_kk=rw_kk, rw_ka=rw_ka, rw_rk=rw_rk, rw_ln_g=rw_ln_g, rw_ln_b=rw_ln_b,
                  w_branch=w_branch, w_out=w_out, p_wq=p_wq, p_key1=p_key1, p_key2=p_key2,
                  p_u=p_u, p_v=p_v, final_norm=final_norm)
    return trunk(x_prompt, x_sample, state_hgrn, state_rwkv, c, c_ctx, params)
```

```python
import functools
import math
import jax
import jax.numpy as jnp
from jax import lax
import numpy as np
from jax.experimental import pallas as pl
from jax.experimental.pallas import tpu as pltpu

D_MODEL = 1024
DEPTH = 4
GRID_W = 64
BRANCH_WIDTH = 512
N_BRANCH = 3

A_DK = 128
A_DV = 128
A_HEADS = BRANCH_WIDTH // A_DV

HY_BANDS = 16
HY_EMB = 1 + 2 * HY_BANDS
HY_FFN = 64

C_HEAD_DIM = 64
C_HEADS = BRANCH_WIDTH // C_HEAD_DIM
C_WIDTH = BRANCH_WIDTH
C_DECAY_LORA = 64
C_AAA_LORA = 64
C_GATE_LORA = 128

A_COLS = 3 * A_HEADS * A_DK + A_HEADS * A_DV + BRANCH_WIDTH
B_COLS = 3 * BRANCH_WIDTH
C_COLS = 3 * C_WIDTH + 2 * C_DECAY_LORA + 2 * C_AAA_LORA + C_GATE_LORA
G_COLS = N_BRANCH * D_MODEL

P_HEADS = 8
P_NKEYS = 128
P_EXPERTS = P_NKEYS * P_NKEYS
P_TOPK = 16
P_KEY_DIM = 128

EPS = 1e-6
RWKV_GN_EPS = 64e-5

LANES = 128
ROW_TILE = 256
VMEM_LIMIT = 56 * 1024 * 1024

F32 = jnp.float32
BF16 = jnp.bfloat16


def _cparams(sem):
    return pltpu.CompilerParams(dimension_semantics=sem, vmem_limit_bytes=VMEM_LIMIT)


def _bdot(a, b):
    return jnp.dot(a.astype(BF16), b.astype(BF16), preferred_element_type=F32)


def _split_hi_lo(x):
    hi = x.astype(BF16)
    lo = (x - hi.astype(F32)).astype(BF16)
    return hi, lo


def _seg_of_tile(i, prompt_tiles, tiles_per_sample):
    return jnp.where(i < prompt_tiles, 0, 1 + (i - prompt_tiles) // tiles_per_sample)


def _norm_mod(x, gain, scale, shift):
    return x * lax.rsqrt(jnp.mean(x * x, axis=-1, keepdims=True) + EPS) * gain * (1.0 + scale) + shift


def _inproj_kernel(x_ref, gain_ref, scale_ref, shift_ref, w_ref, o_ref):
    h = _norm_mod(x_ref[...], gain_ref[...], scale_ref[0], shift_ref[0])
    o_ref[...] = jnp.dot(h.astype(BF16), w_ref[...], preferred_element_type=F32)


def norm_mod_matmul(x, gain, scale, shift, w_bf16, prompt_tiles, tiles_per_sample):
    n, d = x.shape
    cols = w_bf16.shape[1]
    seg = lambda i: (_seg_of_tile(i, prompt_tiles, tiles_per_sample), 0, 0)
    return pl.pallas_call(
        _inproj_kernel,
        grid=(n // ROW_TILE,),
        in_specs=[pl.BlockSpec((ROW_TILE, d), lambda i: (i, 0)),
                  pl.BlockSpec((1, d), lambda i: (0, 0)),
                  pl.BlockSpec((1, 1, d), seg),
                  pl.BlockSpec((1, 1, d), seg),
                  pl.BlockSpec((d, cols), lambda i: (0, 0))],
        out_specs=pl.BlockSpec((ROW_TILE, cols), lambda i: (i, 0)),
        out_shape=jax.ShapeDtypeStruct((n, cols), F32),
        compiler_params=_cparams(("parallel",)),
    )(x, gain.reshape(1, d), scale, shift, w_bf16)


HG_CHUNK = 128
HG_LEVELS = 7


def _hgrn_tables():
    c = HG_CHUNK
    out = np.zeros((2, HG_LEVELS + 2, c, c), np.float32)
    t = np.arange(c)
    for d in range(2):
        pos = t if d == 0 else c - 1 - t
        pt, pu = pos[:, None], pos[None, :]
        for lv in range(HG_LEVELS):
            m = c >> (lv + 1)
            same = (pt // m) == (pu // m)
            query = ((pt // m) % 2) == 1
            out[d, lv] = same & np.where(query, pu <= pt, pu > pt)
        out[d, HG_LEVELS] = pu <= pt
        out[d, HG_LEVELS + 1] = pu > pt
    return out.reshape(2, (HG_LEVELS + 2) * c, c)


def _hgrn_kernel(q_ref, fz_ref, v_ref, lb_ref, tab_ref, s0_ref, o_ref, sfin_ref, st_scr, *, n_chunks):
    c = HG_CHUNK
    d = pl.program_id(1)
    ci = pl.program_id(2)

    @pl.when(ci == 0)
    def _():
        for h in range(A_HEADS):
            st_scr[h] = s0_ref[0, 0, h].T

    row = lax.broadcasted_iota(jnp.int32, (c, LANES), 0)
    pos_r = row + d * (c - 1 - 2 * row)
    ti = lax.broadcasted_iota(jnp.int32, (c, c), 0)
    si = lax.broadcasted_iota(jnp.int32, (c, c), 1)
    pos_t = ti + d * (c - 1 - 2 * ti)
    pos_s = si + d * (c - 1 - 2 * si)
    tab = tab_ref[0]

    for h in range(A_HEADS):
        hs = pl.ds(h * LANES, LANES)
        qz = q_ref[:, hs]
        q = qz * jax.nn.sigmoid(qz)
        lb = lb_ref[0, :, hs]
        f = lb + (1.0 - lb) * jax.nn.sigmoid(fz_ref[:, hs])
        kd = 1.0 - f
        logf = jnp.log(f)
        v = v_ref[:, hs]

        lhi, llo = _split_hi_lo(logf)
        args = (jnp.dot(tab, lhi, preferred_element_type=F32)
                + jnp.dot(tab, llo, preferred_element_type=F32))
        e_all = jnp.exp(args)

        attn = jnp.where(ti == si, jnp.sum(q * kd, axis=-1, keepdims=True), 0.0)
        for lv in range(HG_LEVELS):
            sh = HG_LEVELS - 1 - lv
            e = e_all[lv * c:(lv + 1) * c]
            is_q = ((pos_r >> sh) & 1) == 1
            qs = jnp.where(is_q, q * e, 0.0).astype(BF16)
            ks = jnp.where(is_q, 0.0, kd * e).astype(BF16)
            sc = lax.dot_general(qs, ks, (((1,), (1,)), ((), ())), preferred_element_type=F32)
            attn = attn + jnp.where((pos_t >> (sh + 1)) == (pos_s >> (sh + 1)), sc, 0.0)

        e_cum = e_all[HG_LEVELS * c:(HG_LEVELS + 1) * c]
        e_rev = e_all[(HG_LEVELS + 1) * c:]
        st = st_scr[h]
        o = _bdot(attn, v) + lax.dot_general((q * e_cum).astype(BF16), st.astype(BF16),
                                             (((1,), (1,)), ((), ())), preferred_element_type=F32)
        o_ref[0, 0, :, hs] = o
        dec = jnp.exp(jnp.sum(logf, axis=0, keepdims=True))
        st_new = st * dec + _bdot(v.T, kd * e_rev)
        st_scr[h] = st_new

        @pl.when(ci == n_chunks - 1)
        def _():
            sfin_ref[0, 0, h] = st_new.T


def hgrn2_scan(pa, lb, s0, row_off, batch, seqlen):
    c = HG_CHUNK
    n_chunks = seqlen // c
    base = row_off // c
    hw = A_HEADS * A_DK
    tab = jnp.asarray(_hgrn_tables(), BF16)

    def rows(b, ci, d):
        return base + b * n_chunks + ci + d * (n_chunks - 1 - 2 * ci)

    def orow(b, ci, d):
        return b * n_chunks + ci + d * (n_chunks - 1 - 2 * ci)

    return pl.pallas_call(
        functools.partial(_hgrn_kernel, n_chunks=n_chunks),
        grid=(batch, 2, n_chunks),
        in_specs=[pl.BlockSpec((c, hw), lambda b, d, ci: (rows(b, ci, d), 0)),
                  pl.BlockSpec((c, hw), lambda b, d, ci: (rows(b, ci, d), 1 + d)),
                  pl.BlockSpec((c, hw), lambda b, d, ci: (rows(b, ci, d), 3)),
                  pl.BlockSpec((1, 1, hw), lambda b, d, ci: (d, 0, 0)),
                  pl.BlockSpec((1, (HG_LEVELS + 2) * c, c), lambda b, d, ci: (d, 0, 0)),
                  pl.BlockSpec((1, 1, A_HEADS, A_DK, A_DV), lambda b, d, ci: (b, d, 0, 0, 0))],
        out_specs=[pl.BlockSpec((1, 1, c, hw), lambda b, d, ci: (d, 0, orow(b, ci, d), 0)),
                   pl.BlockSpec((1, 1, A_HEADS, A_DK, A_DV), lambda b, d, ci: (b, d, 0, 0, 0))],
        out_shape=[jax.ShapeDtypeStruct((2, 1, batch * seqlen, A_HEADS * A_DV), F32),
                   jax.ShapeDtypeStruct((batch, 2, A_HEADS, A_DK, A_DV), F32)],
        scratch_shapes=[pltpu.VMEM((A_HEADS, A_DV, A_DK), F32)],
        compiler_params=_cparams(("parallel", "parallel", "arbitrary")),
    )(pa, pa, pa, lb, tab, s0)


def _dft_table_kernel(fc_ref, fs_ref, gc_ref, gs_ref, *, length, tile):
    n2 = 2 * length
    a = lax.broadcasted_iota(jnp.int32, (tile, length), 0) + pl.program_id(0) * tile
    b = lax.broadcasted_iota(jnp.int32, (tile, length), 1)
    ang = ((a * b) & (n2 - 1)).astype(F32) * (2.0 * math.pi / n2)
    co, si = jnp.cos(ang), jnp.sin(ang)
    alt_b = (1 - 2 * (b & 1)).astype(F32)
    alt_a = (1 - 2 * (a & 1)).astype(F32)
    fc_ref[...] = co.astype(BF16)
    fs_ref[...] = jnp.where(a == 0, alt_b, -si).astype(BF16)
    wk = jnp.where(b == 0, 1.0 / n2, 2.0 / n2)
    gc_ref[...] = (wk * co).astype(BF16)
    gs_ref[...] = jnp.where(b == 0, alt_a * (1.0 / n2), -wk * si).astype(BF16)


def dft_tables(length):
    tile = min(length, 256)
    spec = pl.BlockSpec((tile, length), lambda i: (i, 0))
    shp = jax.ShapeDtypeStruct((length, length), BF16)
    return pl.pallas_call(
        functools.partial(_dft_table_kernel, length=length, tile=tile),
        grid=(length // tile,),
        out_specs=[spec] * 4, out_shape=[shp] * 4,
        compiler_params=_cparams(("parallel",)),
    )()


def _dft_fwd_kernel(fc_ref, fs_ref, z_ref, zr_ref, zi_ref):
    z = z_ref[0].astype(BF16)
    zr_ref[0] = jnp.dot(fc_ref[...], z, preferred_element_type=F32)
    zi_ref[0] = jnp.dot(fs_ref[...], z, preferred_element_type=F32)


def dft_forward(fc, fs, z):
    batch, length, ch = z.shape
    tile = min(length, 512)
    return pl.pallas_call(
        _dft_fwd_kernel,
        grid=(length // tile, batch),
        in_specs=[pl.BlockSpec((tile, length), lambda i, b: (i, 0)),
                  pl.BlockSpec((tile, length), lambda i, b: (i, 0)),
                  pl.BlockSpec((1, length, ch), lambda i, b: (b, 0, 0))],
        out_specs=[pl.BlockSpec((1, tile, ch), lambda i, b: (b, i, 0))] * 2,
        out_shape=[jax.ShapeDtypeStruct((batch, length, ch), F32)] * 2,
        compiler_params=_cparams(("parallel", "arbitrary")),
    )(fc, fs, z)


def _spec_mul_kernel(zr_ref, zi_ref, kr_ref, ki_ref, yr_ref, yi_ref, *, tile):
    row = lax.broadcasted_iota(jnp.int32, (tile, 1), 0) + pl.program_id(1) * tile
    sgn = (1 - 2 * (row & 1)).astype(F32)
    kr = kr_ref[0] + sgn * kr_ref[1]
    ki = ki_ref[0] + sgn * ki_ref[1]
    zr, zi = zr_ref[0], zi_ref[0]
    packed = row == 0
    yr_ref[0] = jnp.where(packed, zr * kr, zr * kr - zi * ki).astype(BF16)
    yi_ref[0] = jnp.where(packed, zi * ki, zr * ki + zi * kr).astype(BF16)


def spectrum_multiply(zr, zi, kr, ki):
    batch, length, ch = zr.shape
    tile = min(length, 512)
    zspec = pl.BlockSpec((1, tile, ch), lambda b, i: (b, i, 0))
    kspec = pl.BlockSpec((2, tile, ch), lambda b, i: (0, i, 0))
    return pl.pallas_call(
        functools.partial(_spec_mul_kernel, tile=tile),
        grid=(batch, length // tile),
        in_specs=[zspec, zspec, kspec, kspec],
        out_specs=[zspec, zspec],
        out_shape=[jax.ShapeDtypeStruct((batch, length, ch), BF16)] * 2,
        compiler_params=_cparams(("parallel", "parallel")),
    )(zr, zi, kr, ki)


def _dft_inv_kernel(gc_ref, gs_ref, yr_ref, yi_ref, x0_ref, z_ref, bias_ref, o_ref):
    y = (jnp.dot(gc_ref[...], yr_ref[0], preferred_element_type=F32)
         + jnp.dot(gs_ref[...], yi_ref[0], preferred_element_type=F32))
    o_ref[0] = x0_ref[0] * (y + z_ref[0] * bias_ref[...])


def dft_inverse_gate(gc, gs, yr, yi, x0, z, bias):
    batch, length, ch = yr.shape
    tile = min(length, 512)
    tspec = pl.BlockSpec((1, tile, ch), lambda i, b: (b, i, 0))
    return pl.pallas_call(
        _dft_inv_kernel,
        grid=(length // tile, batch),
        in_specs=[pl.BlockSpec((tile, length), lambda i, b: (i, 0)),
                  pl.BlockSpec((tile, length), lambda i, b: (i, 0)),
                  pl.BlockSpec((1, length, ch), lambda i, b: (b, 0, 0)),
                  pl.BlockSpec((1, length, ch), lambda i, b: (b, 0, 0)),
                  tspec, tspec,
                  pl.BlockSpec((1, ch), lambda i, b: (0, 0))],
        out_specs=tspec,
        out_shape=jax.ShapeDtypeStruct((batch, length, ch), F32),
        compiler_params=_cparams(("parallel", "arbitrary")),
    )(gc, gs, yr, yi, x0, z, bias.reshape(1, ch))


def shift_prev(x):
    return jnp.pad(x, ((0, 0), (1, 0), (0, 0)))[:, :-1]


def shift_next(x):
    return jnp.pad(x, ((0, 0), (0, 1), (0, 0)))[:, 1:]


def hyena_filter(length, lp):
    hp = lax.Precision.HIGHEST
    t = jnp.linspace(0.0, 1.0, length, dtype=F32)[:, None]
    bands = jnp.linspace(1e-4, HY_BANDS - 1, HY_BANDS, dtype=F32)[None, :]
    ang = (2 * math.pi / length) * jnp.arange(length, dtype=F32)[:, None] * bands
    z = jnp.concatenate([t, jnp.cos(ang), -jnp.sin(ang)], axis=-1)
    hf = jnp.sin(lp['hy_freq1'] * (jnp.dot(z, lp['hy_w1'], precision=hp) + lp['hy_b1']))
    hf = jnp.sin(lp['hy_freq2'] * (jnp.dot(hf, lp['hy_w2'], precision=hp) + lp['hy_b2']))
    hf = jnp.dot(hf, lp['hy_w3'], precision=hp).reshape(length, 2, BRANCH_WIDTH)
    hf = hf * jnp.exp(-t * jnp.abs(lp['hy_delta']))[:, None, :]
    lo = hf[:, 0]
    hi = jnp.concatenate([jnp.zeros((1, BRANCH_WIDTH), F32), hf[:0:-1, 1]], axis=0)
    norm = jnp.sum(jnp.abs(lo), axis=0, keepdims=True) + jnp.sum(jnp.abs(hi), axis=0, keepdims=True)
    return jnp.stack([lo, hi]) / norm


def hyena_branch(pb, lp, tables):
    batch, length, _ = pb.shape
    fc, fs, gc, gs = tables
    cw = lp['hy_conv_w']
    u = shift_prev(pb) * cw[0] + pb * cw[1] + shift_next(pb) * cw[2] + lp['hy_conv_b']
    x0, x1, v = jnp.split(u, 3, axis=-1)
    z = v * x1
    kr, ki = dft_forward(fc, fs, hyena_filter(length, lp))
    zr, zi = dft_forward(fc, fs, z)
    yr, yi = spectrum_multiply(zr, zi, kr, ki)
    return dft_inverse_gate(gc, gs, yr, yi, x0, z, lp['hy_bias'])


RW_GROUP_BATCH = 4
RW_VROWS = C_HEAD_DIM // 2
RW_TCHUNK = 32
RW_OUTS = 11


def _head_sum(y, bd):
    hi, lo = _split_hi_lo(y)
    return jnp.dot(hi, bd, preferred_element_type=F32) + jnp.dot(lo, bd, preferred_element_type=F32)


def _head_ones():
    h = np.arange(C_WIDTH) // C_HEAD_DIM
    return jnp.asarray(h[:, None] == h[None, :], BF16)


def _rwkv_pre_kernel(prv_ref, cur_ref, nxt_ref, mu_ref, kks_ref, ka_ref, rk_ref, w0_ref, w2_ref, a0_ref,
                     a2_ref, g2_ref, bd_ref, *out_refs, prompt_tiles, tiles_per_prompt, tiles_per_sample,
                     prompt_len, sample_len):
    i = pl.program_id(0)
    tm = ROW_TILE
    latent = i >= prompt_tiles
    tile_in_seq = jnp.where(latent, (i - prompt_tiles) % tiles_per_sample, i % tiles_per_prompt)
    seq_len = jnp.where(latent, sample_len, prompt_len)
    row = lax.broadcasted_iota(jnp.int32, (tm, 1), 0)
    pos = tile_in_seq * tm + row
    col = pos % GRID_W
    one = lambda c: jnp.where(c, 1.0, 0.0)
    m_left = jnp.where(latent, one(col != 0), one(pos != 0))
    m_right = jnp.where(latent, one(col != GRID_W - 1), one(pos != seq_len - 1))
    m_up = one(pos >= GRID_W)
    m_down = one(pos < seq_len - GRID_W)
    c_lr = jnp.where(latent, 0.25, 0.5)
    c_ud = jnp.where(latent, 0.25, 0.0)

    cur = cur_ref[...]
    prv = prv_ref[...]
    nxt = nxt_ref[...]
    left = jnp.where(row == 0, prv[tm - 1:tm], pltpu.roll(cur, 1, axis=0))
    right = jnp.where(row == tm - 1, nxt[0:1], pltpu.roll(cur, tm - 1, axis=0))
    up = jnp.concatenate([prv[tm - GRID_W:], cur[:tm - GRID_W]], axis=0)
    down = jnp.concatenate([cur[GRID_W:], nxt[:GRID_W]], axis=0)
    nm = c_lr * (left * m_left + right * m_right) + c_ud * (up * m_up + down * m_down)
    x = cur + (nm - cur) * mu_ref[...]

    cw = C_WIDTH
    r, k, v = x[:, :cw], x[:, cw:2 * cw], x[:, 2 * cw:3 * cw]
    o = 3 * cw
    wd = x[:, o:o + 2 * C_DECAY_LORA]
    ad = x[:, o + 2 * C_DECAY_LORA:o + 2 * C_DECAY_LORA + 2 * C_AAA_LORA]
    gd = x[:, o + 2 * C_DECAY_LORA + 2 * C_AAA_LORA:]
    bd = bd_ref[...]

    kk = k * kks_ref[...]
    kkn = kk * lax.rsqrt(_head_sum(kk * kk, bd) + 1e-12)
    r_o, v_o, a_o, wf_o, wb_o, kf_o, kb_o, bf_o, bb_o, g_o, bonus_o = out_refs
    r_o[...] = r
    v_o[...] = v
    a_o[...] = -kkn
    g_o[...] = _bdot(jax.nn.sigmoid(gd), g2_ref[...])
    bonus_o[...] = _head_sum(r * k * rk_ref[...], bd) * v
    for d, (w_o, k_o, b_o) in enumerate(((wf_o, kf_o, bf_o), (wb_o, kb_o, bb_o))):
        z = -(w0_ref[d] + _bdot(jnp.tanh(wd[:, d * C_DECAY_LORA:(d + 1) * C_DECAY_LORA]), w2_ref[d]))
        softplus = jnp.maximum(z, 0.0) + jnp.log(1.0 + jnp.exp(-jnp.abs(z)))
        w_o[...] = jnp.exp(-jnp.exp(-softplus - 0.5))
        ag = jax.nn.sigmoid(a0_ref[d] + _bdot(ad[:, d * C_AAA_LORA:(d + 1) * C_AAA_LORA], a2_ref[d]))
        k_o[...] = k * (1.0 + (ag - 1.0) * ka_ref[...])
        b_o[...] = kkn * ag


def rwkv_pre(pc, lp, prompt_tiles, tiles_per_prompt, tiles_per_sample, prompt_len, sample_len):
    n, cc = pc.shape
    cw = C_WIDTH
    nt = n // ROW_TILE
    vec = lambda a: a.reshape(1, -1)
    const2 = lambda shp: pl.BlockSpec(shp, lambda i: (0, 0))
    const3 = lambda shp: pl.BlockSpec(shp, lambda i: (0, 0, 0))
    ospec = pl.BlockSpec((ROW_TILE, cw), lambda i: (i, 0))
    kern = functools.partial(_rwkv_pre_kernel, prompt_tiles=prompt_tiles, tiles_per_prompt=tiles_per_prompt,
                             tiles_per_sample=tiles_per_sample, prompt_len=prompt_len, sample_len=sample_len)
    return pl.pallas_call(
        kern,
        grid=(nt,),
        in_specs=[pl.BlockSpec((ROW_TILE, cc), lambda i: (jnp.maximum(i - 1, 0), 0)),
                  pl.BlockSpec((ROW_TILE, cc), lambda i: (i, 0)),
                  pl.BlockSpec((ROW_TILE, cc), lambda i: (jnp.minimum(i + 1, nt - 1), 0)),
                  const2((1, cc)), const2((1, cw)), const2((1, cw)), const2((1, cw)),
                  const3((2, 1, cw)), const3((2, C_DECAY_LORA, cw)),
                  const3((2, 1, cw)), const3((2, C_AAA_LORA, cw)),
                  const2((C_GATE_LORA, cw)), const2((cw, cw))],
        out_specs=[ospec] * RW_OUTS,
        out_shape=[jax.ShapeDtypeStruct((n, cw), F32)] * RW_OUTS,
        compiler_params=_cparams(("parallel",)),
    )(pc, pc, pc, vec(lp['rw_mu']), vec(lp['rw_kk']), vec(lp['rw_ka']), vec(lp['rw_rk']),
      lp['rw_w0'][:, None, :], lp['rw_w2'].astype(BF16), lp['rw_a0'][:, None, :], lp['rw_a2'].astype(BF16),
      lp['rw_g2'].astype(BF16), _head_ones())


def _rwkv_scan_kernel(rf_ref, rb_ref, wf_ref, wb_ref, kf_ref, kb_ref, af_ref, ab_ref, bf_ref, bb_ref,
                      vf_ref, vb_ref, s0_ref, yf_ref, yb_ref, sfin_ref, s_scr, row_scr, *, n_tc):
    tc = pl.program_id(1)

    @pl.when(tc == 0)
    def _():
        s_scr[...] = s0_ref[0]

    nk = C_HEAD_DIM
    tch = RW_TCHUNK
    lane_k = lax.broadcasted_iota(jnp.int32, (nk, LANES), 1)
    bwd_k = ((lane_k >> 5) & 1) == 1
    lane_v = lax.broadcasted_iota(jnp.int32, (RW_VROWS, LANES), 1)
    bwd_v = ((lane_v >> 5) & 1) == 1
    pairs = ((rf_ref, rb_ref), (wf_ref, wb_ref), (kf_ref, kb_ref), (af_ref, ab_ref), (bf_ref, bb_ref))
    R, W, KD, A, B = range(5)

    def step(t, carry):
        tb = tch - 1 - t
        for idx, (f_ref, b_ref) in enumerate(pairs):
            row_scr[idx] = jnp.where(bwd_k, b_ref[0, tb], f_ref[0, t])
        v_t = jnp.where(bwd_v, vb_ref[0, tb], vf_ref[0, t])
        parts = [jnp.zeros((RW_VROWS, LANES), F32) for _ in range(4)]
        for kk in range(nk):
            parts[kk % 4] = parts[kk % 4] + s_scr[kk] * row_scr[A, pl.ds(kk, 1), :]
        sa = (parts[0] + parts[1]) + (parts[2] + parts[3])
        ys = [jnp.zeros((RW_VROWS, LANES), F32) for _ in range(4)]
        for kk in range(nk):
            s_new = (s_scr[kk] * row_scr[W, pl.ds(kk, 1), :] + sa * row_scr[B, pl.ds(kk, 1), :]
                     + v_t * row_scr[KD, pl.ds(kk, 1), :])
            s_scr[kk] = s_new
            ys[kk % 4] = ys[kk % 4] + s_new * row_scr[R, pl.ds(kk, 1), :]
        y = (ys[0] + ys[1]) + (ys[2] + ys[3])
        yf_ref[0, t] = y
        yb_ref[0, tb] = y
        return carry

    lax.fori_loop(0, tch, step, 0)

    @pl.when(tc == n_tc - 1)
    def _():
        sfin_ref[0] = s_scr[...]


def rwkv7_scan_lanes(r, w, k, a, b, v, s0):
    g, t = r.shape[0], r.shape[1]
    n_tc = t // RW_TCHUNK
    fwd = lambda gi, ti: (gi, ti, 0, 0)
    bwd = lambda gi, ti: (gi, n_tc - 1 - ti, 0, 0)
    kf = pl.BlockSpec((1, RW_TCHUNK, C_HEAD_DIM, LANES), fwd)
    kb = pl.BlockSpec((1, RW_TCHUNK, C_HEAD_DIM, LANES), bwd)
    vf = pl.BlockSpec((1, RW_TCHUNK, RW_VROWS, LANES), fwd)
    vb = pl.BlockSpec((1, RW_TCHUNK, RW_VROWS, LANES), bwd)
    sspec = pl.BlockSpec((1, C_HEAD_DIM, RW_VROWS, LANES), lambda gi, ti: (gi, 0, 0, 0))
    yshape = jax.ShapeDtypeStruct((g, t, RW_VROWS, LANES), F32)
    return pl.pallas_call(
        functools.partial(_rwkv_scan_kernel, n_tc=n_tc),
        grid=(g, n_tc),
        in_specs=[kf, kb] * 5 + [vf, vb, sspec],
        out_specs=[vf, vb, sspec],
        out_shape=[yshape, yshape, jax.ShapeDtypeStruct((g, C_HEAD_DIM, RW_VROWS, LANES), F32)],
        scratch_shapes=[pltpu.VMEM((C_HEAD_DIM, RW_VROWS, LANES), F32),
                        pltpu.VMEM((5, C_HEAD_DIM, LANES), F32)],
        compiler_params=_cparams(("parallel", "arbitrary")),
    )(r, r, w, w, k, k, a, a, b, b, v, v, s0)


def _rows_to_groups(x, batch, seqlen):
    g = batch // RW_GROUP_BATCH
    x = x.reshape(g, RW_GROUP_BATCH, seqlen, C_HEADS, C_HEAD_DIM)
    return jnp.transpose(x, (0, 2, 4, 1, 3)).reshape(g, seqlen, C_HEAD_DIM, RW_GROUP_BATCH * C_HEADS)


def _key_lanes(xf, xb, batch, seqlen):
    x = jnp.stack([_rows_to_groups(xf, batch, seqlen), _rows_to_groups(xb, batch, seqlen)], axis=3)
    g = x.shape[0]
    x = jnp.broadcast_to(x[:, :, :, None], (g, seqlen, C_HEAD_DIM, 2, 2, x.shape[-1]))
    return x.reshape(g, seqlen, C_HEAD_DIM, LANES)


def _value_lanes(v, batch, seqlen):
    x = _rows_to_groups(v, batch, seqlen)
    g, m = x.shape[0], x.shape[-1]
    x = jnp.transpose(x.reshape(g, seqlen, 2, RW_VROWS, m), (0, 1, 3, 2, 4))
    x = jnp.broadcast_to(x[:, :, :, :, None], (g, seqlen, RW_VROWS, 2, 2, m))
    return x.reshape(g, seqlen, RW_VROWS, LANES)


def _lanes_to_rows(yf, yb, batch, seqlen):
    g = yf.shape[0]
    shp = (g, seqlen, RW_VROWS, 2, 2, RW_GROUP_BATCH, C_HEADS)
    y = yf.reshape(shp)[:, :, :, :, 0] + yb.reshape(shp)[:, :, :, :, 1]
    return jnp.transpose(y, (0, 4, 1, 5, 3, 2)).reshape(batch * seqlen, C_WIDTH)


def _state_to_lanes(s):
    bsz = s.shape[0]
    g = bsz // RW_GROUP_BATCH
    s = s.reshape(g, RW_GROUP_BATCH, 2, C_HEADS, 2, RW_VROWS, C_HEAD_DIM)
    return jnp.transpose(s, (0, 6, 5, 4, 2, 1, 3)).reshape(g, C_HEAD_DIM, RW_VROWS, LANES)


def _state_from_lanes(s):
    g = s.shape[0]
    s = s.reshape(g, C_HEAD_DIM, RW_VROWS, 2, 2, RW_GROUP_BATCH, C_HEADS)
    s = jnp.transpose(s, (0, 5, 4, 6, 3, 2, 1))
    return s.reshape(g * RW_GROUP_BATCH, 2, C_HEADS, C_HEAD_DIM, C_HEAD_DIM)


def rwkv7_scan_segment(pre, s0, off, batch, seqlen):
    r, v, a, wf, wb, kf, kb, bf, bb = [z[off:off + batch * seqlen] for z in pre[:9]]
    args = [_key_lanes(r, r, batch, seqlen), _key_lanes(wf, wb, batch, seqlen),
            _key_lanes(kf, kb, batch, seqlen), _key_lanes(a, a, batch, seqlen),
            _key_lanes(bf, bb, batch, seqlen), _value_lanes(v, batch, seqlen), _state_to_lanes(s0)]
    yf, yb, sfin = rwkv7_scan_lanes(*args)
    return _lanes_to_rows(yf, yb, batch, seqlen), _state_from_lanes(sfin)


def _merge_kernel(x_ref, oa_ref, ga_ref, gn_ref, yb_ref, yr_ref, bonus_ref, rg_ref, lng_ref, lnb_ref, bd_ref,
                  pg_ref, gate_ref, wb_ref, wo_ref, o_ref):
    oa = oa_ref[0, 0] + oa_ref[1, 0]
    gz = ga_ref[...]
    gn = gn_ref[...]
    parts = []
    for h in range(A_HEADS):
        oh = oa[:, h * A_DV:(h + 1) * A_DV]
        parts.append(oh * lax.rsqrt(jnp.mean(oh * oh, axis=-1, keepdims=True) + EPS) * gn)
    ya = jnp.concatenate(parts, axis=-1) * (gz * jax.nn.sigmoid(gz))
    bd = bd_ref[...]
    yr = yr_ref[...]
    inv = 1.0 / C_HEAD_DIM
    yr = yr - _head_sum(yr, bd) * inv
    var = _head_sum(yr * yr, bd) * inv
    yc = (yr * lax.rsqrt(var + RWKV_GN_EPS) * lng_ref[...] + lnb_ref[...] + bonus_ref[...]) * rg_ref[...]
    pg = pg_ref[...]
    d = D_MODEL
    merged = (jax.nn.sigmoid(pg[:, :d]) * _bdot(ya, wb_ref[0])
              + jax.nn.sigmoid(pg[:, d:2 * d]) * _bdot(yb_ref[...], wb_ref[1])
              + jax.nn.sigmoid(pg[:, 2 * d:]) * _bdot(yc, wb_ref[2]))
    o_ref[...] = x_ref[...] + gate_ref[0] * _bdot(merged, wo_ref[...])


def merge_out(x, o_hgrn, pa, hgrn_norm, yb, y_rwkv, bonus, rw_gate, ln_g, ln_b, pg, gate, wb_bf16, wo_bf16,
              prompt_tiles, tiles_per_sample):
    n, d = x.shape
    bw = BRANCH_WIDTH
    row = lambda i: (i, 0)
    seg = lambda i: (_seg_of_tile(i, prompt_tiles, tiles_per_sample), 0, 0)
    rspec = pl.BlockSpec((ROW_TILE, bw), row)
    vspec = pl.BlockSpec((1, bw), lambda i: (0, 0))
    return pl.pallas_call(
        _merge_kernel,
        grid=(n // ROW_TILE,),
        in_specs=[pl.BlockSpec((ROW_TILE, d), row),
                  pl.BlockSpec((2, 1, ROW_TILE, bw), lambda i: (0, 0, i, 0)),
                  pl.BlockSpec((ROW_TILE, bw), lambda i: (i, (A_COLS - bw) // bw)),
                  pl.BlockSpec((1, A_DV), lambda i: (0, 0)),
                  rspec, rspec, rspec, rspec, vspec, vspec,
                  pl.BlockSpec((bw, bw), lambda i: (0, 0)),
                  pl.BlockSpec((ROW_TILE, G_COLS), row),
                  pl.BlockSpec((1, 1, d), seg),
                  pl.BlockSpec((N_BRANCH, bw, d), lambda i: (0, 0, 0)),
                  pl.BlockSpec((d, d), lambda i: (0, 0))],
        out_specs=pl.BlockSpec((ROW_TILE, d), row),
        out_shape=jax.ShapeDtypeStruct((n, d), F32),
        compiler_params=_cparams(("parallel",)),
    )(x, o_hgrn, pa, hgrn_norm.reshape(1, A_DV), yb, y_rwkv, bonus, rw_gate, ln_g.reshape(1, bw),
      ln_b.reshape(1, bw), _head_ones(), pg, gate, wb_bf16, wo_bf16)


PEER_SEL_TILE = 256
PEER_TOK_TILE = 512
PEER_EXP_TILE = 1024
PEER_RANKS = P_TOPK + 1


def _nt_x3_kernel(a_ref, b_ref, hi_ref, lo_ref):
    ahi, alo = _split_hi_lo(a_ref[...])
    bhi, blo = _split_hi_lo(b_ref[...])
    nt = lambda p, q: lax.dot_general(p, q, (((1,), (1,)), ((), ())), preferred_element_type=F32)
    m = nt(ahi, bhi) + (nt(ahi, blo) + nt(alo, bhi))
    hi, lo = _split_hi_lo(m)
    hi_ref[...] = hi
    lo_ref[...] = lo


def peer_score_matrix(p_wq, p_key1, p_key2):
    d = p_wq.shape[0]
    eye = jnp.eye(P_HEADS, dtype=F32)
    k1 = jnp.einsum('nd,hg->hngd', p_key1, eye)
    k2 = jnp.einsum('nd,hg->hngd', p_key2, eye)
    zeros = jnp.zeros_like(k1)
    kbig = jnp.stack([jnp.stack([k1, zeros], axis=3), jnp.stack([zeros, k2], axis=3)])
    kbig = kbig.reshape(2 * P_HEADS * P_NKEYS, P_HEADS * P_KEY_DIM)
    rows = kbig.shape[0]
    tile = 256
    return pl.pallas_call(
        _nt_x3_kernel,
        grid=(rows // tile,),
        in_specs=[pl.BlockSpec((tile, P_HEADS * P_KEY_DIM), lambda i: (i, 0)),
                  pl.BlockSpec((d, P_HEADS * P_KEY_DIM), lambda i: (0, 0))],
        out_specs=[pl.BlockSpec((tile, d), lambda i: (i, 0))] * 2,
        out_shape=[jax.ShapeDtypeStruct((rows, d), BF16)] * 2,
        compiler_params=_cparams(("parallel",)),
    )(kbig, p_wq)


def _peer_scores_kernel(x_ref, gain_ref, scale_ref, shift_ref, mhi_ref, mlo_ref, ht_ref, s_ref):
    h = _norm_mod(x_ref[...], gain_ref[...], scale_ref[0], shift_ref[0])
    ht = h.T
    hhi, hlo = _split_hi_lo(ht)
    ht_ref[...] = hhi
    mhi = mhi_ref[...]
    s_ref[...] = (jnp.dot(mhi, hhi, preferred_element_type=F32)
                  + (jnp.dot(mhi, hlo, preferred_element_type=F32)
                     + jnp.dot(mlo_ref[...], hhi, preferred_element_type=F32)))


def peer_scores(x, gain, scale, shift, mhi, mlo, prompt_tiles, tiles_per_sample):
    n, d = x.shape
    rows = mhi.shape[0]
    seg = lambda i: (_seg_of_tile(i, prompt_tiles, tiles_per_sample), 0, 0)
    return pl.pallas_call(
        _peer_scores_kernel,
        grid=(n // ROW_TILE,),
        in_specs=[pl.BlockSpec((ROW_TILE, d), lambda i: (i, 0)),
                  pl.BlockSpec((1, d), lambda i: (0, 0)),
                  pl.BlockSpec((1, 1, d), seg),
                  pl.BlockSpec((1, 1, d), seg),
                  pl.BlockSpec((rows, d), lambda i: (0, 0)),
                  pl.BlockSpec((rows, d), lambda i: (0, 0))],
        out_specs=[pl.BlockSpec((d, ROW_TILE), lambda i: (0, i)),
                   pl.BlockSpec((rows, ROW_TILE), lambda i: (0, i))],
        out_shape=[jax.ShapeDtypeStruct((d, n), BF16),
                   jax.ShapeDtypeStruct((rows, n), F32)],
        compiler_params=_cparams(("parallel",)),
    )(x, gain.reshape(1, d), scale, shift, mhi, mlo)


def _peer_select_kernel(s_ref, cnt_ref, e1_ref, rank_ref, e2_ref, v1_scr, v2_scr, st_scr):
    neg = -jnp.inf
    nr = PEER_RANKS
    for h in range(P_HEADS):
        for half, scr in ((0, v1_scr), (1, v2_scr)):
            s = s_ref[half, h]
            m = jnp.max(s, axis=0, keepdims=True)
            scr[0, pl.ds(h, 1), :] = m
            for rnk in range(1, nr):
                m = jnp.max(jnp.where(s < m, s, neg), axis=0, keepdims=True)
                scr[rnk, pl.ds(h, 1), :] = m
    cands = []
    for a in range(nr):
        for b in range(nr):
            if (a + 1) * (b + 1) <= nr:
                cands.append(v1_scr[a] + v2_scr[b])
    top = cands[0]
    m = top
    for rnk in range(1, nr):
        prev = m
        m = None
        for cnd in cands:
            x = jnp.where(cnd < prev, cnd, neg)
            m = x if m is None else jnp.maximum(m, x)
        if rnk == nr - 2:
            t16 = m
    t17 = m
    tau = 0.5 * (t16 + t17)
    zsum = jnp.zeros_like(top)
    for cnd in cands:
        zsum = zsum + jnp.where(cnd > tau, jnp.exp(cnd - top), 0.0)
    st_scr[0] = tau
    st_scr[1] = 1.0 / zsum
    for h in range(P_HEADS):
        s1 = s_ref[0, h]
        s2 = s_ref[1, h]
        thr = st_scr[0, pl.ds(h, 1), :] - s1
        cnt = jnp.zeros_like(s1)
        rank = jnp.zeros_like(s2)
        for b in range(nr):
            v2b = v2_scr[b, pl.ds(h, 1), :]
            cnt = cnt + jnp.where(v2b >= thr, 1.0, 0.0)
            rank = rank + jnp.where(v2b > s2, 1.0, 0.0)
        cnt_ref[h] = cnt
        rank_ref[h] = pltpu.bitcast(rank.astype(BF16), jnp.uint32)
        e1_ref[h] = jnp.exp(s1 - v1_scr[0, pl.ds(h, 1), :])
        e2 = jnp.exp(s2 - v2_scr[0, pl.ds(h, 1), :]) * st_scr[1, pl.ds(h, 1), :]
        e2_ref[h] = pltpu.bitcast(e2.astype(BF16), jnp.uint32)


def peer_select(s4):
    n = s4.shape[-1]
    tl = PEER_SEL_TILE
    ospec = pl.BlockSpec((P_HEADS, P_NKEYS, tl), lambda i: (0, 0, i))
    f32s = jax.ShapeDtypeStruct((P_HEADS, P_NKEYS, n), F32)
    pspec = pl.BlockSpec((P_HEADS, P_NKEYS // 2, tl), lambda i: (0, 0, i))
    packed = jax.ShapeDtypeStruct((P_HEADS, P_NKEYS // 2, n), jnp.uint32)
    return pl.pallas_call(
        _peer_select_kernel,
        grid=(n // tl,),
        in_specs=[pl.BlockSpec((2, P_HEADS, P_NKEYS, tl), lambda i: (0, 0, 0, i))],
        out_specs=[ospec, ospec, pspec, pspec], out_shape=[f32s, f32s, packed, packed],
        scratch_shapes=[pltpu.VMEM((PEER_RANKS, P_HEADS, tl), F32),
                        pltpu.VMEM((PEER_RANKS, P_HEADS, tl), F32),
                        pltpu.VMEM((2, P_HEADS, tl), F32)],
        compiler_params=_cparams(("parallel",)),
    )(s4)


def _peer_expert_kernel(ht_ref, u_ref, vt_ref, cnt_ref, e1_ref, rank_ref, e2_ref, x_ref, gate_ref,
                        o_ref, acc_scr, w_scr, *, n_eb):
    j = pl.program_id(1)

    @pl.when(j == 0)
    def _():
        acc_scr[...] = jnp.zeros_like(acc_scr)

    n_i = PEER_EXP_TILE // P_NKEYS
    sub = 16
    n_jt = P_NKEYS // sub
    for il in range(n_i):
        for lt in range(PEER_TOK_TILE // LANES):
            ls = pl.ds(lt * LANES, LANES)
            w = [None] * n_jt
            for h in range(P_HEADS):
                cb = jnp.broadcast_to(cnt_ref[h, pl.ds(il, 1), ls], (sub, LANES)).astype(BF16)
                eb = jnp.broadcast_to(e1_ref[h, pl.ds(il, 1), ls], (sub, LANES)).astype(BF16)
                for jt in range(n_jt):
                    rs = pl.ds(jt * (sub // 2), sub // 2)
                    rank = pltpu.bitcast(rank_ref[h, rs, ls], BF16)
                    e2 = pltpu.bitcast(e2_ref[h, rs, ls], BF16)
                    sel = jnp.where(rank < cb, e2, jnp.zeros((), BF16)) * eb
                    w[jt] = sel if w[jt] is None else w[jt] + sel
            for jt in range(n_jt):
                w_scr[pl.ds(il * P_NKEYS + jt * sub, sub), ls] = w[jt]
    at = jnp.dot(u_ref[...], ht_ref[...], preferred_element_type=F32)
    p = w_scr[...] * jax.nn.gelu(at).astype(BF16)
    acc_scr[...] += jnp.dot(vt_ref[...], p, preferred_element_type=F32)

    @pl.when(j == n_eb - 1)
    def _():
        o_ref[...] = x_ref[...] + gate_ref[0] * acc_scr[...].T


def peer_experts(ht, u_bf16, vt_bf16, cnt, e1, rank, e2, x, gate, prompt_tiles, tiles_per_sample):
    n, d = x.shape
    tt, eb = PEER_TOK_TILE, PEER_EXP_TILE
    n_eb = P_EXPERTS // eb
    n_i = eb // P_NKEYS
    scale = tt // ROW_TILE
    seg = lambda i, j: (_seg_of_tile(i * scale, prompt_tiles, tiles_per_sample), 0, 0)
    full = pl.BlockSpec((P_HEADS, P_NKEYS // 2, tt), lambda i, j: (0, 0, i))
    part = pl.BlockSpec((P_HEADS, n_i, tt), lambda i, j: (0, j, i))
    return pl.pallas_call(
        functools.partial(_peer_expert_kernel, n_eb=n_eb),
        grid=(n // tt, n_eb),
        in_specs=[pl.BlockSpec((d, tt), lambda i, j: (0, i)),
                  pl.BlockSpec((eb, d), lambda i, j: (j, 0)),
                  pl.BlockSpec((d, eb), lambda i, j: (0, j)),
                  part, part, full, full,
                  pl.BlockSpec((tt, d), lambda i, j: (i, 0)),
                  pl.BlockSpec((1, 1, d), seg)],
        out_specs=pl.BlockSpec((tt, d), lambda i, j: (i, 0)),
        out_shape=jax.ShapeDtypeStruct((n, d), F32),
        scratch_shapes=[pltpu.VMEM((d, tt), F32), pltpu.VMEM((eb, tt), BF16)],
        compiler_params=_cparams(("parallel", "arbitrary")),
    )(ht, u_bf16, vt_bf16, cnt, e1, rank, e2, x, gate)


def _final_norm_kernel(x_ref, g_ref, o_ref):
    x = x_ref[...]
    o_ref[...] = x * lax.rsqrt(jnp.mean(x * x, axis=-1, keepdims=True) + EPS) * g_ref[...]


def final_norm(x, gain):
    n, d = x.shape
    return pl.pallas_call(
        _final_norm_kernel,
        grid=(n // ROW_TILE,),
        in_specs=[pl.BlockSpec((ROW_TILE, d), lambda i: (i, 0)), pl.BlockSpec((1, d), lambda i: (0, 0))],
        out_specs=pl.BlockSpec((ROW_TILE, d), lambda i: (i, 0)),
        out_shape=jax.ShapeDtypeStruct((n, d), F32),
        compiler_params=_cparams(("parallel",)),
    )(x, gain.reshape(1, d))


def trunk(x_prompt, x_sample, state_hgrn, state_rwkv, c, c_ctx, params):
    bp, tp, d = x_prompt.shape
    bs, ts, _ = x_sample.shape
    n_p, n_s = bp * tp, bs * ts
    prompt_tiles, tiles_per_sample = n_p // ROW_TILE, ts // ROW_TILE
    tiling = (prompt_tiles, tiles_per_sample)
    x = jnp.concatenate([x_prompt.reshape(n_p, d), x_sample.reshape(n_s, d)], axis=0)
    cvec = jnp.concatenate([c_ctx[None], c], axis=0)

    lb_p = jax.nn.softmax(params['hgrn_lb_logits'], axis=1)
    lb = jnp.cumsum(lb_p, axis=1)
    lb = lb - lb[:, :1]
    tables = {tp: dft_tables(tp), ts: dft_tables(ts)}
    zero_a = jnp.zeros((bp, 2, A_HEADS, A_DK, A_DV), F32)
    zero_c = jnp.zeros((bp, 2, C_HEADS, C_HEAD_DIM, C_HEAD_DIM), F32)

    new_a, new_c = [], []
    for l in range(DEPTH):
        lp = {k: v[l] for k, v in params.items() if k not in ('hgrn_lb_logits', 'final_norm')}
        mod = jnp.dot(jax.nn.silu(cvec), lp['w_ada'], precision=lax.Precision.HIGHEST) + lp['b_ada']
        sh1, sc1, g1, sh2, sc2, g2 = [m[:, None, :] for m in jnp.split(mod, 6, axis=-1)]

        w_in = lp['w_in'].astype(BF16)
        splits = np.cumsum([0, A_COLS, B_COLS, C_COLS, G_COLS])
        pa, pb, pc, pg = [norm_mod_matmul(x, lp['norm1'], sc1, sh1, w_in[:, splits[i]:splits[i + 1]], *tiling)
                          for i in range(4)]

        lb_l = lb[:, l][:, None, :]
        oa_p, sa_p = hgrn2_scan(pa, lb_l, zero_a, 0, bp, tp)
        oa_s, _ = hgrn2_scan(pa, lb_l, state_hgrn[:, l], n_p, bs, ts)
        o_hgrn = jnp.concatenate([oa_p, oa_s], axis=2)

        yb = jnp.concatenate([hyena_branch(pb[:n_p].reshape(bp, tp, B_COLS), lp, tables[tp]).reshape(n_p, -1),
                              hyena_branch(pb[n_p:].reshape(bs, ts, B_COLS), lp, tables[ts]).reshape(n_s, -1)])

        pre = rwkv_pre(pc, lp, prompt_tiles, tp // ROW_TILE, tiles_per_sample, tp, ts)
        yr_p, sc_p = rwkv7_scan_segment(pre, zero_c, 0, bp, tp)
        yr_s, _ = rwkv7_scan_segment(pre, state_rwkv[:, l], n_p, bs, ts)
        y_rwkv = jnp.concatenate([yr_p, yr_s])

        x = merge_out(x, o_hgrn, pa, lp['hgrn_norm'], yb, y_rwkv, pre[10], pre[9], lp['rw_ln_g'], lp['rw_ln_b'],
                      pg, g1, lp['w_branch'].astype(BF16), lp['w_out'].astype(BF16), *tiling)

        mhi, mlo = peer_score_matrix(lp['p_wq'], lp['p_key1'], lp['p_key2'])
        ht, st = peer_scores(x, lp['norm2'], sc2, sh2, mhi, mlo, *tiling)
        s4 = st.reshape(2, P_HEADS, P_NKEYS, n_p + n_s)
        cnt, e1, rank, e2 = peer_select(s4)
        x = peer_experts(ht, lp['p_u'].astype(BF16), lp['p_v'].T.astype(BF16), cnt, e1, rank, e2, x, g2, *tiling)

        new_a.append(sa_p)
        new_c.append(sc_p)

    y = final_norm(x, params['final_norm'])
    return (y[:n_p].reshape(bp, tp, d), y[n_p:].reshape(bs, ts, d),
            jnp.stack(new_a, axis=1), jnp.stack(new_c, axis=1))


def kernel(x_prompt, x_sample, state_hgrn, state_rwkv, c, c_ctx, w_ada, b_ada, norm1, norm2, w_in,
           hgrn_lb_logits, hgrn_norm, hy_conv_w, hy_conv_b, hy_w1, hy_b1, hy_freq1, hy_w2, hy_b2,
           hy_freq2, hy_w3, hy_delta, hy_bias, rw_mu, rw_w0, rw_w2, rw_a0, rw_a2, rw_g2, rw_kk, rw_ka,
           rw_rk, rw_ln_g, rw_ln_b, w_branch, w_out, p_wq, p_key1, p_key2, p_u, p_v, final_norm):
    params = dict(w_ada=w_ada, b_ada=b_ada, norm1=norm1, norm2=norm2, w_in=w_in,
                  hgrn_lb_logits=hgrn_lb_logits, hgrn_norm=hgrn_norm, hy_conv_w=hy_conv_w,
                  hy_conv_b=hy_conv_b, hy_w1=hy_w1, hy_b1=hy_b1, hy_freq1=hy_freq1, hy_w2=hy_w2,
                  hy_b2=hy_b2, hy_freq2=hy_freq2, hy_w3=hy_w3, hy_delta=hy_delta, hy_bias=hy_bias,
                  rw_mu=rw_mu, rw_w0=rw_w0, rw_w2=rw_w2, rw_a0=rw_a0, rw_a2=rw_a2, rw_g2=rw_g2,
                  rw_kk=rw_kk, rw_ka=rw_ka, rw_rk=rw_rk, rw_ln_g=rw_ln_g, rw_ln_b=rw_ln_b,
                  w_branch=w_branch, w_out=w_out, p_wq=p_wq, p_key1=p_key1, p_key2=p_key2,
                  p_u=p_u, p_v=p_v, final_norm=final_norm)
    return trunk(x_prompt, x_sample, state_hgrn, state_rwkv, c, c_ctx, params)
```

```python
import functools
import math
import jax
import jax.numpy as jnp
from jax import lax
import numpy as np
from jax.experimental import pallas as pl
from jax.experimental.pallas import tpu as pltpu

D_MODEL = 1024
DEPTH = 4
GRID_W = 64
BRANCH_WIDTH = 512
N_BRANCH = 3

A_DK = 128
A_DV = 128
A_HEADS = BRANCH_WIDTH // A_DV

HY_BANDS = 16
HY_EMB = 1 + 2 * HY_BANDS
HY_FFN = 64

C_HEAD_DIM = 64
C_HEADS = BRANCH_WIDTH // C_HEAD_DIM
C_WIDTH = BRANCH_WIDTH
C_DECAY_LORA = 64
C_AAA_LORA = 64
C_GATE_LORA = 128

A_COLS = 3 * A_HEADS * A_DK + A_HEADS * A_DV + BRANCH_WIDTH
B_COLS = 3 * BRANCH_WIDTH
C_COLS = 3 * C_WIDTH + 2 * C_DECAY_LORA + 2 * C_AAA_LORA + C_GATE_LORA
G_COLS = N_BRANCH * D_MODEL

P_HEADS = 8
P_NKEYS = 128
P_EXPERTS = P_NKEYS * P_NKEYS
P_TOPK = 16
P_KEY_DIM = 128

EPS = 1e-6
RWKV_GN_EPS = 64e-5

LANES = 128
ROW_TILE = 256
VMEM_LIMIT = 56 * 1024 * 1024

F32 = jnp.float32
BF16 = jnp.bfloat16


def _cparams(sem):
    return pltpu.CompilerParams(dimension_semantics=sem, vmem_limit_bytes=VMEM_LIMIT)


def _bdot(a, b):
    return jnp.dot(a.astype(BF16), b.astype(BF16), preferred_element_type=F32)


def _split_hi_lo(x):
    hi = x.astype(BF16)
    lo = (x - hi.astype(F32)).astype(BF16)
    return hi, lo


def _seg_of_tile(i, prompt_tiles, tiles_per_sample):
    return jnp.where(i < prompt_tiles, 0, 1 + (i - prompt_tiles) // tiles_per_sample)


def _norm_mod(x, gain, scale, shift):
    return x * lax.rsqrt(jnp.mean(x * x, axis=-1, keepdims=True) + EPS) * gain * (1.0 + scale) + shift


def _inproj_kernel(x_ref, gain_ref, scale_ref, shift_ref, w_ref, o_ref):
    h = _norm_mod(x_ref[...], gain_ref[...], scale_ref[0], shift_ref[0])
    o_ref[...] = jnp.dot(h.astype(BF16), w_ref[...], preferred_element_type=F32)


def norm_mod_matmul(x, gain, scale, shift, w_bf16, prompt_tiles, tiles_per_sample):
    n, d = x.shape
    cols = w_bf16.shape[1]
    seg = lambda i: (_seg_of_tile(i, prompt_tiles, tiles_per_sample), 0, 0)
    return pl.pallas_call(
        _inproj_kernel,
        grid=(n // ROW_TILE,),
        in_specs=[pl.BlockSpec((ROW_TILE, d), lambda i: (i, 0)),
                  pl.BlockSpec((1, d), lambda i: (0, 0)),
                  pl.BlockSpec((1, 1, d), seg),
                  pl.BlockSpec((1, 1, d), seg),
                  pl.BlockSpec((d, cols), lambda i: (0, 0))],
        out_specs=pl.BlockSpec((ROW_TILE, cols), lambda i: (i, 0)),
        out_shape=jax.ShapeDtypeStruct((n, cols), F32),
        compiler_params=_cparams(("parallel",)),
    )(x, gain.reshape(1, d), scale, shift, w_bf16)


HG_CHUNK = 128
HG_LEVELS = 7


def _hgrn_tables():
    c = HG_CHUNK
    out = np.zeros((2, HG_LEVELS + 2, c, c), np.float32)
    t = np.arange(c)
    for d in range(2):
        pos = t if d == 0 else c - 1 - t
        pt, pu = pos[:, None], pos[None, :]
        for lv in range(HG_LEVELS):
            m = c >> (lv + 1)
            same = (pt // m) == (pu // m)
            query = ((pt // m) % 2) == 1
            out[d, lv] = same & np.where(query, pu <= pt, pu > pt)
        out[d, HG_LEVELS] = pu <= pt
        out[d, HG_LEVELS + 1] = pu > pt
    return out.reshape(2, (HG_LEVELS + 2) * c, c)


def _hgrn_kernel(q_ref, fz_ref, v_ref, lb_ref, tab_ref, s0_ref, o_ref, sfin_ref, st_scr, *, n_chunks):
    c = HG_CHUNK
    d = pl.program_id(1)
    ci = pl.program_id(2)

    @pl.when(ci == 0)
    def _():
        for h in range(A_HEADS):
            st_scr[h] = s0_ref[0, 0, h].T

    row = lax.broadcasted_iota(jnp.int32, (c, LANES), 0)
    pos_r = row + d * (c - 1 - 2 * row)
    ti = lax.broadcasted_iota(jnp.int32, (c, c), 0)
    si = lax.broadcasted_iota(jnp.int32, (c, c), 1)
    pos_t = ti + d * (c - 1 - 2 * ti)
    pos_s = si + d * (c - 1 - 2 * si)
    tab = tab_ref[0]

    for h in range(A_HEADS):
        hs = pl.ds(h * LANES, LANES)
        qz = q_ref[:, hs]
        q = qz * jax.nn.sigmoid(qz)
        lb = lb_ref[0, :, hs]
        f = lb + (1.0 - lb) * jax.nn.sigmoid(fz_ref[:, hs])
        kd = 1.0 - f
        logf = jnp.log(f)
        v = v_ref[:, hs]

        lhi, llo = _split_hi_lo(logf)
        args = (jnp.dot(tab, lhi, preferred_element_type=F32)
                + jnp.dot(tab, llo, preferred_element_type=F32))
        e_all = jnp.exp(args)

        attn = jnp.where(ti == si, jnp.sum(q * kd, axis=-1, keepdims=True), 0.0)
        for lv in range(HG_LEVELS):
            sh = HG_LEVELS - 1 - lv
            e = e_all[lv * c:(lv + 1) * c]
            is_q = ((pos_r >> sh) & 1) == 1
            qs = jnp.where(is_q, q * e, 0.0).astype(BF16)
            ks = jnp.where(is_q, 0.0, kd * e).astype(BF16)
            sc = lax.dot_general(qs, ks, (((1,), (1,)), ((), ())), preferred_element_type=F32)
            attn = attn + jnp.where((pos_t >> (sh + 1)) == (pos_s >> (sh + 1)), sc, 0.0)

        e_cum = e_all[HG_LEVELS * c:(HG_LEVELS + 1) * c]
        e_rev = e_all[(HG_LEVELS + 1) * c:]
        st = st_scr[h]
        o = _bdot(attn, v) + lax.dot_general((q * e_cum).astype(BF16), st.astype(BF16),
                                             (((1,), (1,)), ((), ())), preferred_element_type=F32)
        o_ref[0, 0, :, hs] = o
        dec = jnp.exp(jnp.sum(logf, axis=0, keepdims=True))
        st_new = st * dec + _bdot(v.T, kd * e_rev)
        st_scr[h] = st_new

        @pl.when(ci == n_chunks - 1)
        def _():
            sfin_ref[0, 0, h] = st_new.T


def hgrn2_scan(pa, lb, s0, row_off, batch, seqlen):
    c = HG_CHUNK
    n_chunks = seqlen // c
    base = row_off // c
    hw = A_HEADS * A_DK
    tab = jnp.asarray(_hgrn_tables(), BF16)

    def rows(b, ci, d):
        return base + b * n_chunks + ci + d * (n_chunks - 1 - 2 * ci)

    def orow(b, ci, d):
        return b * n_chunks + ci + d * (n_chunks - 1 - 2 * ci)

    return pl.pallas_call(
        functools.partial(_hgrn_kernel, n_chunks=n_chunks),
        grid=(batch, 2, n_chunks),
        in_specs=[pl.BlockSpec((c, hw), lambda b, d, ci: (rows(b, ci, d), 0)),
                  pl.BlockSpec((c, hw), lambda b, d, ci: (rows(b, ci, d), 1 + d)),
                  pl.BlockSpec((c, hw), lambda b, d, ci: (rows(b, ci, d), 3)),
                  pl.BlockSpec((1, 1, hw), lambda b, d, ci: (d, 0, 0)),
                  pl.BlockSpec((1, (HG_LEVELS + 2) * c, c), lambda b, d, ci: (d, 0, 0)),
                  pl.BlockSpec((1, 1, A_HEADS, A_DK, A_DV), lambda b, d, ci: (b, d, 0, 0, 0))],
        out_specs=[pl.BlockSpec((1, 1, c, hw), lambda b, d, ci: (d, 0, orow(b, ci, d), 0)),
                   pl.BlockSpec((1, 1, A_HEADS, A_DK, A_DV), lambda b, d, ci: (b, d, 0, 0, 0))],
        out_shape=[jax.ShapeDtypeStruct((2, 1, batch * seqlen, A_HEADS * A_DV), F32),
                   jax.ShapeDtypeStruct((batch, 2, A_HEADS, A_DK, A_DV), F32)],
        scratch_shapes=[pltpu.VMEM((A_HEADS, A_DV, A_DK), F32)],
        compiler_params=_cparams(("parallel", "parallel", "arbitrary")),
    )(pa, pa, pa, lb, tab, s0)


def _dft_table_kernel(fc_ref, fs_ref, gc_ref, gs_ref, *, length, tile):
    n2 = 2 * length
    a = lax.broadcasted_iota(jnp.int32, (tile, length), 0) + pl.program_id(0) * tile
    b = lax.broadcasted_iota(jnp.int32, (tile, length), 1)
    ang = ((a * b) & (n2 - 1)).astype(F32) * (2.0 * math.pi / n2)
    co, si = jnp.cos(ang), jnp.sin(ang)
    alt_b = (1 - 2 * (b & 1)).astype(F32)
    alt_a = (1 - 2 * (a & 1)).astype(F32)
    fc_ref[...] = co.astype(BF16)
    fs_ref[...] = jnp.where(a == 0, alt_b, -si).astype(BF16)
    wk = jnp.where(b == 0, 1.0 / n2, 2.0 / n2)
    gc_ref[...] = (wk * co).astype(BF16)
    gs_ref[...] = jnp.where(b == 0, alt_a * (1.0 / n2), -wk * si).astype(BF16)


def dft_tables(length):
    tile = min(length, 256)
    spec = pl.BlockSpec((tile, length), lambda i: (i, 0))
    shp = jax.ShapeDtypeStruct((length, length), BF16)
    return pl.pallas_call(
        functools.partial(_dft_table_kernel, length=length, tile=tile),
        grid=(length // tile,),
        out_specs=[spec] * 4, out_shape=[shp] * 4,
        compiler_params=_cparams(("parallel",)),
    )()


def _dft_fwd_kernel(fc_ref, fs_ref, z_ref, zr_ref, zi_ref):
    z = z_ref[0].astype(BF16)
    zr_ref[0] = jnp.dot(fc_ref[...], z, preferred_element_type=F32)
    zi_ref[0] = jnp.dot(fs_ref[...], z, preferred_element_type=F32)


def dft_forward(fc, fs, z):
    batch, length, ch = z.shape
    tile = min(length, 512)
    return pl.pallas_call(
        _dft_fwd_kernel,
        grid=(length // tile, batch),
        in_specs=[pl.BlockSpec((tile, length), lambda i, b: (i, 0)),
                  pl.BlockSpec((tile, length), lambda i, b: (i, 0)),
                  pl.BlockSpec((1, length, ch), lambda i, b: (b, 0, 0))],
        out_specs=[pl.BlockSpec((1, tile, ch), lambda i, b: (b, i, 0))] * 2,
        out_shape=[jax.ShapeDtypeStruct((batch, length, ch), F32)] * 2,
        compiler_params=_cparams(("parallel", "arbitrary")),
    )(fc, fs, z)


def _spec_mul_kernel(zr_ref, zi_ref, kr_ref, ki_ref, yr_ref, yi_ref, *, tile):
    row = lax.broadcasted_iota(jnp.int32, (tile, 1), 0) + pl.program_id(1) * tile
    sgn = (1 - 2 * (row & 1)).astype(F32)
    kr = kr_ref[0] + sgn * kr_ref[1]
    ki = ki_ref[0] + sgn * ki_ref[1]
    zr, zi = zr_ref[0], zi_ref[0]
    packed = row == 0
    yr_ref[0] = jnp.where(packed, zr * kr, zr * kr - zi * ki).astype(BF16)
    yi_ref[0] = jnp.where(packed, zi * ki, zr * ki + zi * kr).astype(BF16)


def spectrum_multiply(zr, zi, kr, ki):
    batch, length, ch = zr.shape
    tile = min(length, 512)
    zspec = pl.BlockSpec((1, tile, ch), lambda b, i: (b, i, 0))
    kspec = pl.BlockSpec((2, tile, ch), lambda b, i: (0, i, 0))
    return pl.pallas_call(
        functools.partial(_spec_mul_kernel, tile=tile),
        grid=(batch, length // tile),
        in_specs=[zspec, zspec, kspec, kspec],
        out_specs=[zspec, zspec],
        out_shape=[jax.ShapeDtypeStruct((batch, length, ch), BF16)] * 2,
        compiler_params=_cparams(("parallel", "parallel")),
    )(zr, zi, kr, ki)


def _dft_inv_kernel(gc_ref, gs_ref, yr_ref, yi_ref, x0_ref, z_ref, bias_ref, o_ref):
    y = (jnp.dot(gc_ref[...], yr_ref[0], preferred_element_type=F32)
         + jnp.dot(gs_ref[...], yi_ref[0], preferred_element_type=F32))
    o_ref[0] = x0_ref[0] * (y + z_ref[0] * bias_ref[...])


def dft_inverse_gate(gc, gs, yr, yi, x0, z, bias):
    batch, length, ch = yr.shape
    tile = min(length, 512)
    tspec = pl.BlockSpec((1, tile, ch), lambda i, b: (b, i, 0))
    return pl.pallas_call(
        _dft_inv_kernel,
        grid=(length // tile, batch),
        in_specs=[pl.BlockSpec((tile, length), lambda i, b: (i, 0)),
                  pl.BlockSpec((tile, length), lambda i, b: (i, 0)),
                  pl.BlockSpec((1, length, ch), lambda i, b: (b, 0, 0)),
                  pl.BlockSpec((1, length, ch), lambda i, b: (b, 0, 0)),
                  tspec, tspec,
                  pl.BlockSpec((1, ch), lambda i, b: (0, 0))],
        out_specs=tspec,
        out_shape=jax.ShapeDtypeStruct((batch, length, ch), F32),
        compiler_params=_cparams(("parallel", "arbitrary")),
    )(gc, gs, yr, yi, x0, z, bias.reshape(1, ch))


def shift_prev(x):
    return jnp.pad(x, ((0, 0), (1, 0), (0, 0)))[:, :-1]


def shift_next(x):
    return jnp.pad(x, ((0, 0), (0, 1), (0, 0)))[:, 1:]


def hyena_filter(length, lp):
    hp = lax.Precision.HIGHEST
    t = jnp.linspace(0.0, 1.0, length, dtype=F32)[:, None]
    bands = jnp.linspace(1e-4, HY_BANDS - 1, HY_BANDS, dtype=F32)[None, :]
    ang = (2 * math.pi / length) * jnp.arange(length, dtype=F32)[:, None] * bands
    z = jnp.concatenate([t, jnp.cos(ang), -jnp.sin(ang)], axis=-1)
    hf = jnp.sin(lp['hy_freq1'] * (jnp.dot(z, lp['hy_w1'], precision=hp) + lp['hy_b1']))
    hf = jnp.sin(lp['hy_freq2'] * (jnp.dot(hf, lp['hy_w2'], precision=hp) + lp['hy_b2']))
    hf = jnp.dot(hf, lp['hy_w3'], precision=hp).reshape(length, 2, BRANCH_WIDTH)
    hf = hf * jnp.exp(-t * jnp.abs(lp['hy_delta']))[:, None, :]
    lo = hf[:, 0]
    hi = jnp.concatenate([jnp.zeros((1, BRANCH_WIDTH), F32), hf[:0:-1, 1]], axis=0)
    norm = jnp.sum(jnp.abs(lo), axis=0, keepdims=True) + jnp.sum(jnp.abs(hi), axis=0, keepdims=True)
    return jnp.stack([lo, hi]) / norm


def hyena_branch(pb, lp, tables):
    batch, length, _ = pb.shape
    fc, fs, gc, gs = tables
    cw = lp['hy_conv_w']
    u = shift_prev(pb) * cw[0] + pb * cw[1] + shift_next(pb) * cw[2] + lp['hy_conv_b']
    x0, x1, v = jnp.split(u, 3, axis=-1)
    z = v * x1
    kr, ki = dft_forward(fc, fs, hyena_filter(length, lp))
    zr, zi = dft_forward(fc, fs, z)
    yr, yi = spectrum_multiply(zr, zi, kr, ki)
    return dft_inverse_gate(gc, gs, yr, yi, x0, z, lp['hy_bias'])


RW_GROUP_BATCH = 4
RW_VROWS = C_HEAD_DIM // 2
RW_TCHUNK = 32
RW_STACK = 9


def _head_sum(y, bd):
    hi, lo = _split_hi_lo(y)
    return jnp.dot(hi, bd, preferred_element_type=F32) + jnp.dot(lo, bd, preferred_element_type=F32)


def _head_ones():
    h = np.arange(C_WIDTH) % C_HEADS
    return jnp.asarray(h[:, None] == h[None, :], BF16)


RW_PERM = np.arange(C_WIDTH).reshape(C_HEADS, C_HEAD_DIM).T.reshape(-1)


def rwkv_permuted_params(lp, w_c):
    p = RW_PERM
    cols = np.concatenate([p, C_WIDTH + p, 2 * C_WIDTH + p, np.arange(3 * C_WIDTH, C_COLS)])
    return dict(w_c=w_c[:, cols], rw_mu=lp['rw_mu'][cols], rw_kk=lp['rw_kk'][p], rw_ka=lp['rw_ka'][p],
                rw_rk=lp['rw_rk'].reshape(-1)[p], rw_w0=lp['rw_w0'][:, p], rw_w2=lp['rw_w2'][:, :, p],
                rw_a0=lp['rw_a0'][:, p], rw_a2=lp['rw_a2'][:, :, p], rw_g2=lp['rw_g2'][:, p],
                rw_ln_g=lp['rw_ln_g'][p], rw_ln_b=lp['rw_ln_b'][p], w_branch_c=lp['w_branch'][2][p])


def _rwkv_pre_kernel(prv_ref, cur_ref, nxt_ref, mu_ref, kks_ref, ka_ref, rk_ref, w0_ref, w2_ref, a0_ref,
                     a2_ref, g2_ref, bd_ref, *out_refs, prompt_tiles, tiles_per_prompt, tiles_per_sample,
                     prompt_len, sample_len):
    i = pl.program_id(0)
    tm = ROW_TILE
    latent = i >= prompt_tiles
    tile_in_seq = jnp.where(latent, (i - prompt_tiles) % tiles_per_sample, i % tiles_per_prompt)
    seq_len = jnp.where(latent, sample_len, prompt_len)
    row = lax.broadcasted_iota(jnp.int32, (tm, 1), 0)
    pos = tile_in_seq * tm + row
    col = pos % GRID_W
    one = lambda c: jnp.where(c, 1.0, 0.0)
    m_left = jnp.where(latent, one(col != 0), one(pos != 0))
    m_right = jnp.where(latent, one(col != GRID_W - 1), one(pos != seq_len - 1))
    m_up = one(pos >= GRID_W)
    m_down = one(pos < seq_len - GRID_W)
    c_lr = jnp.where(latent, 0.25, 0.5)
    c_ud = jnp.where(latent, 0.25, 0.0)

    cur = cur_ref[...]
    prv = prv_ref[...]
    nxt = nxt_ref[...]
    left = jnp.where(row == 0, prv[tm - 1:tm], pltpu.roll(cur, 1, axis=0))
    right = jnp.where(row == tm - 1, nxt[0:1], pltpu.roll(cur, tm - 1, axis=0))
    up = jnp.concatenate([prv[tm - GRID_W:], cur[:tm - GRID_W]], axis=0)
    down = jnp.concatenate([cur[GRID_W:], nxt[:GRID_W]], axis=0)
    nm = c_lr * (left * m_left + right * m_right) + c_ud * (up * m_up + down * m_down)
    x = cur + (nm - cur) * mu_ref[...]

    cw = C_WIDTH
    r, k, v = x[:, :cw], x[:, cw:2 * cw], x[:, 2 * cw:3 * cw]
    o = 3 * cw
    wd = x[:, o:o + 2 * C_DECAY_LORA]
    ad = x[:, o + 2 * C_DECAY_LORA:o + 2 * C_DECAY_LORA + 2 * C_AAA_LORA]
    gd = x[:, o + 2 * C_DECAY_LORA + 2 * C_AAA_LORA:]
    bd = bd_ref[...]

    kk = k * kks_ref[...]
    kkn = kk * lax.rsqrt(_head_sum(kk * kk, bd) + 1e-12)
    pt_o, g_o, bonus_o = out_refs
    pt_o[0] = r.T
    pt_o[1] = v.T
    pt_o[2] = (-kkn).T
    g_o[...] = _bdot(jax.nn.sigmoid(gd), g2_ref[...])
    bonus_o[...] = _head_sum(r * k * rk_ref[...], bd) * v
    for d in range(2):
        z = -(w0_ref[d] + _bdot(jnp.tanh(wd[:, d * C_DECAY_LORA:(d + 1) * C_DECAY_LORA]), w2_ref[d]))
        softplus = jnp.maximum(z, 0.0) + jnp.log(1.0 + jnp.exp(-jnp.abs(z)))
        pt_o[3 + d] = jnp.exp(-jnp.exp(-softplus - 0.5)).T
        ag = jax.nn.sigmoid(a0_ref[d] + _bdot(ad[:, d * C_AAA_LORA:(d + 1) * C_AAA_LORA], a2_ref[d]))
        pt_o[5 + d] = (k * (1.0 + (ag - 1.0) * ka_ref[...])).T
        pt_o[7 + d] = (kkn * ag).T


def rwkv_pre(pc, rp, prompt_tiles, tiles_per_prompt, tiles_per_sample, prompt_len, sample_len):
    n, cc = pc.shape
    cw = C_WIDTH
    nt = n // ROW_TILE
    vec = lambda a: a.reshape(1, -1)
    const2 = lambda shp: pl.BlockSpec(shp, lambda i: (0, 0))
    const3 = lambda shp: pl.BlockSpec(shp, lambda i: (0, 0, 0))
    ospec = pl.BlockSpec((ROW_TILE, cw), lambda i: (i, 0))
    kern = functools.partial(_rwkv_pre_kernel, prompt_tiles=prompt_tiles, tiles_per_prompt=tiles_per_prompt,
                             tiles_per_sample=tiles_per_sample, prompt_len=prompt_len, sample_len=sample_len)
    return pl.pallas_call(
        kern,
        grid=(nt,),
        in_specs=[pl.BlockSpec((ROW_TILE, cc), lambda i: (jnp.maximum(i - 1, 0), 0)),
                  pl.BlockSpec((ROW_TILE, cc), lambda i: (i, 0)),
                  pl.BlockSpec((ROW_TILE, cc), lambda i: (jnp.minimum(i + 1, nt - 1), 0)),
                  const2((1, cc)), const2((1, cw)), const2((1, cw)), const2((1, cw)),
                  const3((2, 1, cw)), const3((2, C_DECAY_LORA, cw)),
                  const3((2, 1, cw)), const3((2, C_AAA_LORA, cw)),
                  const2((C_GATE_LORA, cw)), const2((cw, cw))],
        out_specs=[pl.BlockSpec((RW_STACK, cw, ROW_TILE), lambda i: (0, 0, i)), ospec, ospec],
        out_shape=[jax.ShapeDtypeStruct((RW_STACK, cw, n), F32), jax.ShapeDtypeStruct((n, cw), F32),
                   jax.ShapeDtypeStruct((n, cw), F32)],
        compiler_params=_cparams(("parallel",)),
    )(pc, pc, pc, vec(rp['rw_mu']), vec(rp['rw_kk']), vec(rp['rw_ka']), vec(rp['rw_rk']),
      rp['rw_w0'][:, None, :], rp['rw_w2'].astype(BF16), rp['rw_a0'][:, None, :], rp['rw_a2'].astype(BF16),
      rp['rw_g2'].astype(BF16), _head_ones())


RW_KSRC = ((0, 0), (3, 4), (5, 6), (2, 2), (7, 8))
RW_RELAYOUT_T = 128
RW_KQ = 4


def _rwkv_lanes_kernel(p0, p1, p2, p3, kl_ref, vl_ref):
    q = pl.program_id(2)
    ps = (p0, p1, p2, p3)
    nk = C_HEAD_DIM // RW_KQ
    for kl in range(nk):
        r0 = pl.multiple_of((q * nk + kl) * C_HEADS, C_HEADS)
        for ai, srcs in enumerate(RW_KSRC):
            base = [ps[b][src, pl.ds(r0, C_HEADS), :] for src in srcs for b in range(RW_GROUP_BATCH)]
            kl_ref[0, ai, kl] = jnp.concatenate(base + base, axis=0).T
    nv = RW_VROWS // RW_KQ
    for vi in range(nv):
        pieces = []
        for vh in range(2):
            r0 = pl.multiple_of((vh * RW_VROWS + q * nv + vi) * C_HEADS, C_HEADS)
            half = [ps[b][1, pl.ds(r0, C_HEADS), :] for b in range(RW_GROUP_BATCH)]
            pieces += half + half
        vl_ref[0, vi] = jnp.concatenate(pieces, axis=0).T


def rwkv_to_lanes(pt, off, batch, seqlen):
    g = batch // RW_GROUP_BATCH
    tt = RW_RELAYOUT_T
    nj = seqlen // tt
    col0 = off // tt
    nk, nv = C_HEAD_DIM // RW_KQ, RW_VROWS // RW_KQ

    def src(b):
        return pl.BlockSpec((RW_STACK, C_WIDTH, tt),
                            lambda gi, j, q: (0, 0, col0 + (gi * RW_GROUP_BATCH + b) * nj + j))

    return pl.pallas_call(
        _rwkv_lanes_kernel,
        grid=(g, nj, RW_KQ),
        in_specs=[src(b) for b in range(RW_GROUP_BATCH)],
        out_specs=[pl.BlockSpec((1, 5, nk, tt, LANES), lambda gi, j, q: (gi, 0, q, j, 0)),
                   pl.BlockSpec((1, nv, tt, LANES), lambda gi, j, q: (gi, q, j, 0))],
        out_shape=[jax.ShapeDtypeStruct((g, 5, C_HEAD_DIM, seqlen, LANES), F32),
                   jax.ShapeDtypeStruct((g, RW_VROWS, seqlen, LANES), F32)],
        compiler_params=_cparams(("parallel", "parallel", "arbitrary")),
    )(pt, pt, pt, pt)


def _rwkv_scan_kernel(kf_ref, kb_ref, vf_ref, vb_ref, s0_ref, yf_ref, yb_ref, sfin_ref, s_scr, row_scr, *, n_tc):
    tc = pl.program_id(1)

    @pl.when(tc == 0)
    def _():
        s_scr[...] = s0_ref[0]

    nk = C_HEAD_DIM
    tch = RW_TCHUNK
    lane_k = lax.broadcasted_iota(jnp.int32, (nk, LANES), 1)
    bwd_k = ((lane_k >> 5) & 1) == 1
    lane_v = lax.broadcasted_iota(jnp.int32, (RW_VROWS, LANES), 1)
    bwd_v = ((lane_v >> 5) & 1) == 1
    R, W, KD, A, B = range(5)

    def step(t, carry):
        tb = tch - 1 - t
        for idx in range(5):
            row_scr[idx] = jnp.where(bwd_k, kb_ref[0, idx, :, tb, :], kf_ref[0, idx, :, t, :])
        v_t = jnp.where(bwd_v, vb_ref[0, :, tb, :], vf_ref[0, :, t, :])
        parts = [jnp.zeros((RW_VROWS, LANES), F32) for _ in range(4)]
        for kk in range(nk):
            parts[kk % 4] = parts[kk % 4] + s_scr[kk] * row_scr[A, pl.ds(kk, 1), :]
        sa = (parts[0] + parts[1]) + (parts[2] + parts[3])
        ys = [jnp.zeros((RW_VROWS, LANES), F32) for _ in range(4)]
        for kk in range(nk):
            s_new = (s_scr[kk] * row_scr[W, pl.ds(kk, 1), :] + sa * row_scr[B, pl.ds(kk, 1), :]
                     + v_t * row_scr[KD, pl.ds(kk, 1), :])
            s_scr[kk] = s_new
            ys[kk % 4] = ys[kk % 4] + s_new * row_scr[R, pl.ds(kk, 1), :]
        y = (ys[0] + ys[1]) + (ys[2] + ys[3])
        yf_ref[0, :, t, :] = y
        yb_ref[0, :, tb, :] = y
        return carry

    lax.fori_loop(0, tch, step, 0)

    @pl.when(tc == n_tc - 1)
    def _():
        sfin_ref[0] = s_scr[...]


def rwkv7_scan_lanes(kl, vl, s0):
    g, t = vl.shape[0], vl.shape[2]
    n_tc = t // RW_TCHUNK
    kf = pl.BlockSpec((1, 5, C_HEAD_DIM, RW_TCHUNK, LANES), lambda gi, ti: (gi, 0, 0, ti, 0))
    kb = pl.BlockSpec((1, 5, C_HEAD_DIM, RW_TCHUNK, LANES), lambda gi, ti: (gi, 0, 0, n_tc - 1 - ti, 0))
    vf = pl.BlockSpec((1, RW_VROWS, RW_TCHUNK, LANES), lambda gi, ti: (gi, 0, ti, 0))
    vb = pl.BlockSpec((1, RW_VROWS, RW_TCHUNK, LANES), lambda gi, ti: (gi, 0, n_tc - 1 - ti, 0))
    sspec = pl.BlockSpec((1, C_HEAD_DIM, RW_VROWS, LANES), lambda gi, ti: (gi, 0, 0, 0))
    yshape = jax.ShapeDtypeStruct((g, RW_VROWS, t, LANES), F32)
    return pl.pallas_call(
        functools.partial(_rwkv_scan_kernel, n_tc=n_tc),
        grid=(g, n_tc),
        in_specs=[kf, kb, vf, vb, sspec],
        out_specs=[vf, vb, sspec],
        out_shape=[yshape, yshape, jax.ShapeDtypeStruct((g, C_HEAD_DIM, RW_VROWS, LANES), F32)],
        scratch_shapes=[pltpu.VMEM((C_HEAD_DIM, RW_VROWS, LANES), F32),
                        pltpu.VMEM((5, C_HEAD_DIM, LANES), F32)],
        compiler_params=_cparams(("parallel", "arbitrary")),
    )(kl, kl, vl, vl, s0)


def _rwkv_unlanes_kernel(yf_ref, yb_ref, o_ref):
    half = LANES // 2
    for vl in range(RW_VROWS):
        tf = yf_ref[0, vl].T
        tb = yb_ref[0, vl].T
        for vh in range(2):
            for b in range(RW_GROUP_BATCH):
                rf = vh * half + b * C_HEADS
                rb = rf + half // 2
                o_ref[0, b, pl.ds((vh * RW_VROWS + vl) * C_HEADS, C_HEADS), :] = (
                    tf[rf:rf + C_HEADS] + tb[rb:rb + C_HEADS])


def rwkv_from_lanes(yf, yb, batch, seqlen):
    g = batch // RW_GROUP_BATCH
    tt = RW_RELAYOUT_T
    yspec = pl.BlockSpec((1, RW_VROWS, tt, LANES), lambda gi, j: (gi, 0, j, 0))
    out = pl.pallas_call(
        _rwkv_unlanes_kernel,
        grid=(g, seqlen // tt),
        in_specs=[yspec, yspec],
        out_specs=pl.BlockSpec((1, RW_GROUP_BATCH, C_WIDTH, tt), lambda gi, j: (gi, 0, 0, j)),
        out_shape=jax.ShapeDtypeStruct((g, RW_GROUP_BATCH, C_WIDTH, seqlen), F32),
        compiler_params=_cparams(("parallel", "parallel")),
    )(yf, yb)
    return out.reshape(batch, C_WIDTH, seqlen)


def _state_to_lanes(s):
    bsz = s.shape[0]
    g = bsz // RW_GROUP_BATCH
    s = s.reshape(g, RW_GROUP_BATCH, 2, C_HEADS, 2, RW_VROWS, C_HEAD_DIM)
    return jnp.transpose(s, (0, 6, 5, 4, 2, 1, 3)).reshape(g, C_HEAD_DIM, RW_VROWS, LANES)


def _state_from_lanes(s):
    g = s.shape[0]
    s = s.reshape(g, C_HEAD_DIM, RW_VROWS, 2, 2, RW_GROUP_BATCH, C_HEADS)
    s = jnp.transpose(s, (0, 5, 4, 6, 3, 2, 1))
    return s.reshape(g * RW_GROUP_BATCH, 2, C_HEADS, C_HEAD_DIM, C_HEAD_DIM)


def rwkv7_scan_segment(pt, s0, off, batch, seqlen):
    kl, vl = rwkv_to_lanes(pt, off, batch, seqlen)
    yf, yb, sfin = rwkv7_scan_lanes(kl, vl, _state_to_lanes(s0))
    return rwkv_from_lanes(yf, yb, batch, seqlen), _state_from_lanes(sfin)


def _merge_kernel(x_ref, oa_ref, ga_ref, gn_ref, yb_ref, yrp_ref, yrs_ref, bonus_ref, rg_ref, lng_ref, lnb_ref,
                  bd_ref, pg_ref, gate_ref, wb_ref, wo_ref, o_ref, *, prompt_tiles):
    oa = oa_ref[0, 0] + oa_ref[1, 0]
    gz = ga_ref[...]
    gn = gn_ref[...]
    parts = []
    for h in range(A_HEADS):
        oh = oa[:, h * A_DV:(h + 1) * A_DV]
        parts.append(oh * lax.rsqrt(jnp.mean(oh * oh, axis=-1, keepdims=True) + EPS) * gn)
    ya = jnp.concatenate(parts, axis=-1) * (gz * jax.nn.sigmoid(gz))
    bd = bd_ref[...]
    yr = jnp.where(pl.program_id(0) < prompt_tiles, yrp_ref[0], yrs_ref[0]).T
    inv = 1.0 / C_HEAD_DIM
    yr = yr - _head_sum(yr, bd) * inv
    var = _head_sum(yr * yr, bd) * inv
    yc = (yr * lax.rsqrt(var + RWKV_GN_EPS) * lng_ref[...] + lnb_ref[...] + bonus_ref[...]) * rg_ref[...]
    pg = pg_ref[...]
    d = D_MODEL
    merged = (jax.nn.sigmoid(pg[:, :d]) * _bdot(ya, wb_ref[0])
              + jax.nn.sigmoid(pg[:, d:2 * d]) * _bdot(yb_ref[...], wb_ref[1])
              + jax.nn.sigmoid(pg[:, 2 * d:]) * _bdot(yc, wb_ref[2]))
    o_ref[...] = x_ref[...] + gate_ref[0] * _bdot(merged, wo_ref[...])


def merge_out(x, o_hgrn, pa, hgrn_norm, yb, yr_prompt, yr_sample, bonus, rw_gate, ln_g, ln_b, pg, gate, wb_bf16,
              wo_bf16, prompt_tiles, tiles_per_sample):
    n, d = x.shape
    bw = BRANCH_WIDTH
    row = lambda i: (i, 0)
    seg = lambda i: (_seg_of_tile(i, prompt_tiles, tiles_per_sample), 0, 0)
    rspec = pl.BlockSpec((ROW_TILE, bw), row)
    vspec = pl.BlockSpec((1, bw), lambda i: (0, 0))
    tiles_per_prompt = yr_prompt.shape[2] // ROW_TILE
    last_p = prompt_tiles - 1

    def prompt_blk(i):
        ic = jnp.minimum(i, last_p)
        return (ic // tiles_per_prompt, 0, ic % tiles_per_prompt)

    def sample_blk(i):
        ic = jnp.maximum(i - prompt_tiles, 0)
        return (ic // tiles_per_sample, 0, ic % tiles_per_sample)

    return pl.pallas_call(
        functools.partial(_merge_kernel, prompt_tiles=prompt_tiles),
        grid=(n // ROW_TILE,),
        in_specs=[pl.BlockSpec((ROW_TILE, d), row),
                  pl.BlockSpec((2, 1, ROW_TILE, bw), lambda i: (0, 0, i, 0)),
                  pl.BlockSpec((ROW_TILE, bw), lambda i: (i, (A_COLS - bw) // bw)),
                  pl.BlockSpec((1, A_DV), lambda i: (0, 0)),
                  rspec,
                  pl.BlockSpec((1, bw, ROW_TILE), prompt_blk),
                  pl.BlockSpec((1, bw, ROW_TILE), sample_blk),
                  rspec, rspec, vspec, vspec,
                  pl.BlockSpec((bw, bw), lambda i: (0, 0)),
                  pl.BlockSpec((ROW_TILE, G_COLS), row),
                  pl.BlockSpec((1, 1, d), seg),
                  pl.BlockSpec((N_BRANCH, bw, d), lambda i: (0, 0, 0)),
                  pl.BlockSpec((d, d), lambda i: (0, 0))],
        out_specs=pl.BlockSpec((ROW_TILE, d), row),
        out_shape=jax.ShapeDtypeStruct((n, d), F32),
        compiler_params=_cparams(("parallel",)),
    )(x, o_hgrn, pa, hgrn_norm.reshape(1, A_DV), yb, yr_prompt, yr_sample, bonus, rw_gate, ln_g.reshape(1, bw),
      ln_b.reshape(1, bw), _head_ones(), pg, gate, wb_bf16, wo_bf16)


PEER_SEL_TILE = 256
PEER_TOK_TILE = 512
PEER_EXP_TILE = 1024
PEER_RANKS = P_TOPK + 1


def _nt_x3_kernel(a_ref, b_ref, hi_ref, lo_ref):
    ahi, alo = _split_hi_lo(a_ref[...])
    bhi, blo = _split_hi_lo(b_ref[...])
    nt = lambda p, q: lax.dot_general(p, q, (((1,), (1,)), ((), ())), preferred_element_type=F32)
    m = nt(ahi, bhi) + (nt(ahi, blo) + nt(alo, bhi))
    hi, lo = _split_hi_lo(m)
    hi_ref[...] = hi
    lo_ref[...] = lo


def peer_score_matrix(p_wq, p_key1, p_key2):
    d = p_wq.shape[0]
    eye = jnp.eye(P_HEADS, dtype=F32)
    k1 = jnp.einsum('nd,hg->hngd', p_key1, eye)
    k2 = jnp.einsum('nd,hg->hngd', p_key2, eye)
    zeros = jnp.zeros_like(k1)
    kbig = jnp.stack([jnp.stack([k1, zeros], axis=3), jnp.stack([zeros, k2], axis=3)])
    kbig = kbig.reshape(2 * P_HEADS * P_NKEYS, P_HEADS * P_KEY_DIM)
    rows = kbig.shape[0]
    tile = 256
    return pl.pallas_call(
        _nt_x3_kernel,
        grid=(rows // tile,),
        in_specs=[pl.BlockSpec((tile, P_HEADS * P_KEY_DIM), lambda i: (i, 0)),
                  pl.BlockSpec((d, P_HEADS * P_KEY_DIM), lambda i: (0, 0))],
        out_specs=[pl.BlockSpec((tile, d), lambda i: (i, 0))] * 2,
        out_shape=[jax.ShapeDtypeStruct((rows, d), BF16)] * 2,
        compiler_params=_cparams(("parallel",)),
    )(kbig, p_wq)


def _peer_scores_kernel(x_ref, gain_ref, scale_ref, shift_ref, mhi_ref, mlo_ref, ht_ref, s_ref):
    h = _norm_mod(x_ref[...], gain_ref[...], scale_ref[0], shift_ref[0])
    ht = h.T
    hhi, hlo = _split_hi_lo(ht)
    ht_ref[...] = hhi
    mhi = mhi_ref[...]
    s_ref[...] = (jnp.dot(mhi, hhi, preferred_element_type=F32)
                  + (jnp.dot(mhi, hlo, preferred_element_type=F32)
                     + jnp.dot(mlo_ref[...], hhi, preferred_element_type=F32)))


def peer_scores(x, gain, scale, shift, mhi, mlo, prompt_tiles, tiles_per_sample):
    n, d = x.shape
    rows = mhi.shape[0]
    seg = lambda i: (_seg_of_tile(i, prompt_tiles, tiles_per_sample), 0, 0)
    return pl.pallas_call(
        _peer_scores_kernel,
        grid=(n // ROW_TILE,),
        in_specs=[pl.BlockSpec((ROW_TILE, d), lambda i: (i, 0)),
                  pl.BlockSpec((1, d), lambda i: (0, 0)),
                  pl.BlockSpec((1, 1, d), seg),
                  pl.BlockSpec((1, 1, d), seg),
                  pl.BlockSpec((rows, d), lambda i: (0, 0)),
                  pl.BlockSpec((rows, d), lambda i: (0, 0))],
        out_specs=[pl.BlockSpec((d, ROW_TILE), lambda i: (0, i)),
                   pl.BlockSpec((rows, ROW_TILE), lambda i: (0, i))],
        out_shape=[jax.ShapeDtypeStruct((d, n), BF16),
                   jax.ShapeDtypeStruct((rows, n), F32)],
        compiler_params=_cparams(("parallel",)),
    )(x, gain.reshape(1, d), scale, shift, mhi, mlo)


def _peer_select_kernel(s_ref, cnt_ref, e1_ref, rank_ref, e2_ref, v1_scr, v2_scr, st_scr):
    neg = -jnp.inf
    nr = PEER_RANKS
    for h in range(P_HEADS):
        for half, scr in ((0, v1_scr), (1, v2_scr)):
            s = s_ref[half, h]
            m = jnp.max(s, axis=0, keepdims=True)
            scr[0, pl.ds(h, 1), :] = m
            for rnk in range(1, nr):
                m = jnp.max(jnp.where(s < m, s, neg), axis=0, keepdims=True)
                scr[rnk, pl.ds(h, 1), :] = m
    cands = []
    for a in range(nr):
        for b in range(nr):
            if (a + 1) * (b + 1) <= nr:
                cands.append(v1_scr[a] + v2_scr[b])
    top = cands[0]
    m = top
    for rnk in range(1, nr):
        prev = m
        m = None
        for cnd in cands:
            x = jnp.where(cnd < prev, cnd, neg)
            m = x if m is None else jnp.maximum(m, x)
        if rnk == nr - 2:
            t16 = m
    t17 = m
    tau = 0.5 * (t16 + t17)
    zsum = jnp.zeros_like(top)
    for cnd in cands:
        zsum = zsum + jnp.where(cnd > tau, jnp.exp(cnd - top), 0.0)
    st_scr[0] = tau
    st_scr[1] = 1.0 / zsum
    for h in range(P_HEADS):
        s1 = s_ref[0, h]
        s2 = s_ref[1, h]
        thr = st_scr[0, pl.ds(h, 1), :] - s1
        cnt = jnp.zeros_like(s1)
        rank = jnp.zeros_like(s2)
        for b in range(nr):
            v2b = v2_scr[b, pl.ds(h, 1), :]
            cnt = cnt + jnp.where(v2b >= thr, 1.0, 0.0)
            rank = rank + jnp.where(v2b > s2, 1.0, 0.0)
        cnt_ref[h] = cnt
        rank_ref[h] = pltpu.bitcast(rank.astype(BF16), jnp.uint32)
        e1_ref[h] = jnp.exp(s1 - v1_scr[0, pl.ds(h, 1), :])
        e2 = jnp.exp(s2 - v2_scr[0, pl.ds(h, 1), :]) * st_scr[1, pl.ds(h, 1), :]
        e2_ref[h] = pltpu.bitcast(e2.astype(BF16), jnp.uint32)


def peer_select(s4):
    n = s4.shape[-1]
    tl = PEER_SEL_TILE
    ospec = pl.BlockSpec((P_HEADS, P_NKEYS, tl), lambda i: (0, 0, i))
    f32s = jax.ShapeDtypeStruct((P_HEADS, P_NKEYS, n), F32)
    pspec = pl.BlockSpec((P_HEADS, P_NKEYS // 2, tl), lambda i: (0, 0, i))
    packed = jax.ShapeDtypeStruct((P_HEADS, P_NKEYS // 2, n), jnp.uint32)
    return pl.pallas_call(
        _peer_select_kernel,
        grid=(n // tl,),
        in_specs=[pl.BlockSpec((2, P_HEADS, P_NKEYS, tl), lambda i: (0, 0, 0, i))],
        out_specs=[ospec, ospec, pspec, pspec], out_shape=[f32s, f32s, packed, packed],
        scratch_shapes=[pltpu.VMEM((PEER_RANKS, P_HEADS, tl), F32),
                        pltpu.VMEM((PEER_RANKS, P_HEADS, tl), F32),
                        pltpu.VMEM((2, P_HEADS, tl), F32)],
        compiler_params=_cparams(("parallel",)),
    )(s4)


def _peer_expert_kernel(ht_ref, u_ref, vt_ref, cnt_ref, e1_ref, rank_ref, e2_ref, x_ref, gate_ref,
                        o_ref, acc_scr, w_scr, *, n_eb):
    j = pl.program_id(1)

    @pl.when(j == 0)
    def _():
        acc_scr[...] = jnp.zeros_like(acc_scr)

    n_i = PEER_EXP_TILE // P_NKEYS
    sub = 16
    n_jt = P_NKEYS // sub
    for il in range(n_i):
        for lt in range(PEER_TOK_TILE // LANES):
            ls = pl.ds(lt * LANES, LANES)
            w = [None] * n_jt
            for h in range(P_HEADS):
                cb = jnp.broadcast_to(cnt_ref[h, pl.ds(il, 1), ls], (sub, LANES)).astype(BF16)
                eb = jnp.broadcast_to(e1_ref[h, pl.ds(il, 1), ls], (sub, LANES)).astype(BF16)
                for jt in range(n_jt):
                    rs = pl.ds(jt * (sub // 2), sub // 2)
                    rank = pltpu.bitcast(rank_ref[h, rs, ls], BF16)
                    e2 = pltpu.bitcast(e2_ref[h, rs, ls], BF16)
                    sel = jnp.where(rank < cb, e2, jnp.zeros((), BF16)) * eb
                    w[jt] = sel if w[jt] is None else w[jt] + sel
            for jt in range(n_jt):
                w_scr[pl.ds(il * P_NKEYS + jt * sub, sub), ls] = w[jt]
    at = jnp.dot(u_ref[...], ht_ref[...], preferred_element_type=F32)
    p = w_scr[...] * jax.nn.gelu(at).astype(BF16)
    acc_scr[...] += jnp.dot(vt_ref[...], p, preferred_element_type=F32)

    @pl.when(j == n_eb - 1)
    def _():
        o_ref[...] = x_ref[...] + gate_ref[0] * acc_scr[...].T


def peer_experts(ht, u_bf16, vt_bf16, cnt, e1, rank, e2, x, gate, prompt_tiles, tiles_per_sample):
    n, d = x.shape
    tt, eb = PEER_TOK_TILE, PEER_EXP_TILE
    n_eb = P_EXPERTS // eb
    n_i = eb // P_NKEYS
    scale = tt // ROW_TILE
    seg = lambda i, j: (_seg_of_tile(i * scale, prompt_tiles, tiles_per_sample), 0, 0)
    full = pl.BlockSpec((P_HEADS, P_NKEYS // 2, tt), lambda i, j: (0, 0, i))
    part = pl.BlockSpec((P_HEADS, n_i, tt), lambda i, j: (0, j, i))
    return pl.pallas_call(
        functools.partial(_peer_expert_kernel, n_eb=n_eb),
        grid=(n // tt, n_eb),
        in_specs=[pl.BlockSpec((d, tt), lambda i, j: (0, i)),
                  pl.BlockSpec((eb, d), lambda i, j: (j, 0)),
                  pl.BlockSpec((d, eb), lambda i, j: (0, j)),
                  part, part, full, full,
                  pl.BlockSpec((tt, d), lambda i, j: (i, 0)),
                  pl.BlockSpec((1, 1, d), seg)],
        out_specs=pl.BlockSpec((tt, d), lambda i, j: (i, 0)),
        out_shape=jax.ShapeDtypeStruct((n, d), F32),
        scratch_shapes=[pltpu.VMEM((d, tt), F32), pltpu.VMEM((eb, tt), BF16)],
        compiler_params=_cparams(("parallel", "arbitrary")),
    )(ht, u_bf16, vt_bf16, cnt, e1, rank, e2, x, gate)


def _final_norm_kernel(x_ref, g_ref, o_ref):
    x = x_ref[...]
    o_ref[...] = x * lax.rsqrt(jnp.mean(x * x, axis=-1, keepdims=True) + EPS) * g_ref[...]


def final_norm(x, gain):
    n, d = x.shape
    return pl.pallas_call(
        _final_norm_kernel,
        grid=(n // ROW_TILE,),
        in_specs=[pl.BlockSpec((ROW_TILE, d), lambda i: (i, 0)), pl.BlockSpec((1, d), lambda i: (0, 0))],
        out_specs=pl.BlockSpec((ROW_TILE, d), lambda i: (i, 0)),
        out_shape=jax.ShapeDtypeStruct((n, d), F32),
        compiler_params=_cparams(("parallel",)),
    )(x, gain.reshape(1, d))


def trunk(x_prompt, x_sample, state_hgrn, state_rwkv, c, c_ctx, params):
    bp, tp, d = x_prompt.shape
    bs, ts, _ = x_sample.shape
    n_p, n_s = bp * tp, bs * ts
    prompt_tiles, tiles_per_sample = n_p // ROW_TILE, ts // ROW_TILE
    tiling = (prompt_tiles, tiles_per_sample)
    x = jnp.concatenate([x_prompt.reshape(n_p, d), x_sample.reshape(n_s, d)], axis=0)
    cvec = jnp.concatenate([c_ctx[None], c], axis=0)

    lb_p = jax.nn.softmax(params['hgrn_lb_logits'], axis=1)
    lb = jnp.cumsum(lb_p, axis=1)
    lb = lb - lb[:, :1]
    tables = {tp: dft_tables(tp), ts: dft_tables(ts)}
    zero_a = jnp.zeros((bp, 2, A_HEADS, A_DK, A_DV), F32)
    zero_c = jnp.zeros((bp, 2, C_HEADS, C_HEAD_DIM, C_HEAD_DIM), F32)

    new_a, new_c = [], []
    for l in range(DEPTH):
        lp = {k: v[l] for k, v in params.items() if k not in ('hgrn_lb_logits', 'final_norm')}
        mod = jnp.dot(jax.nn.silu(cvec), lp['w_ada'], precision=lax.Precision.HIGHEST) + lp['b_ada']
        sh1, sc1, g1, sh2, sc2, g2 = [m[:, None, :] for m in jnp.split(mod, 6, axis=-1)]

        w_in = lp['w_in'].astype(BF16)
        splits = np.cumsum([0, A_COLS, B_COLS, C_COLS, G_COLS])
        rp = rwkv_permuted_params(lp, w_in[:, splits[2]:splits[3]])
        w_parts = [w_in[:, splits[0]:splits[1]], w_in[:, splits[1]:splits[2]], rp['w_c'], w_in[:, splits[3]:]]
        pa, pb, pc, pg = [norm_mod_matmul(x, lp['norm1'], sc1, sh1, w, *tiling) for w in w_parts]

        lb_l = lb[:, l][:, None, :]
        oa_p, sa_p = hgrn2_scan(pa, lb_l, zero_a, 0, bp, tp)
        oa_s, _ = hgrn2_scan(pa, lb_l, state_hgrn[:, l], n_p, bs, ts)
        o_hgrn = jnp.concatenate([oa_p, oa_s], axis=2)

        yb = jnp.concatenate([hyena_branch(pb[:n_p].reshape(bp, tp, B_COLS), lp, tables[tp]).reshape(n_p, -1),
                              hyena_branch(pb[n_p:].reshape(bs, ts, B_COLS), lp, tables[ts]).reshape(n_s, -1)])

        pt, rw_gate, bonus = rwkv_pre(pc, rp, prompt_tiles, tp // ROW_TILE, tiles_per_sample, tp, ts)
        yr_p, sc_p = rwkv7_scan_segment(pt, zero_c, 0, bp, tp)
        yr_s, _ = rwkv7_scan_segment(pt, state_rwkv[:, l], n_p, bs, ts)

        wb = jnp.stack([lp['w_branch'][0], lp['w_branch'][1], rp['w_branch_c']]).astype(BF16)
        x = merge_out(x, o_hgrn, pa, lp['hgrn_norm'], yb, yr_p, yr_s, bonus, rw_gate, rp['rw_ln_g'], rp['rw_ln_b'],
                      pg, g1, wb, lp['w_out'].astype(BF16), *tiling)

        mhi, mlo = peer_score_matrix(lp['p_wq'], lp['p_key1'], lp['p_key2'])
        ht, st = peer_scores(x, lp['norm2'], sc2, sh2, mhi, mlo, *tiling)
        s4 = st.reshape(2, P_HEADS, P_NKEYS, n_p + n_s)
        cnt, e1, rank, e2 = peer_select(s4)
        x = peer_experts(ht, lp['p_u'].astype(BF16), lp['p_v'].T.astype(BF16), cnt, e1, rank, e2, x, g2, *tiling)

        new_a.append(sa_p)
        new_c.append(sc_p)

    y = final_norm(x, params['final_norm'])
    return (y[:n_p].reshape(bp, tp, d), y[n_p:].reshape(bs, ts, d),
            jnp.stack(new_a, axis=1), jnp.stack(new_c, axis=1))


def kernel(x_prompt, x_sample, state_hgrn, state_rwkv, c, c_ctx, w_ada, b_ada, norm1, norm2, w_in,
           hgrn_lb_logits, hgrn_norm, hy_conv_w, hy_conv_b, hy_w1, hy_b1, hy_freq1, hy_w2, hy_b2,
           hy_freq2, hy_w3, hy_delta, hy_bias, rw_mu, rw_w0, rw_w2, rw_a0, rw_a2, rw_g2, rw_kk, rw_ka,
           rw_rk, rw_ln_g, rw_ln_b, w_branch, w_out, p_wq, p_key1, p_key2, p_u, p_v, final_norm):
    params = dict(w_ada=w_ada, b_ada=b_ada, norm1=norm1, norm2=norm2, w_in=w_in,
                  hgrn_lb_logits=hgrn_lb_logits, hgrn_norm=hgrn_norm, hy_conv_w=hy_conv_w,
                  hy_conv_b=hy_conv_b, hy_w1=hy_w1, hy_b1=hy_b1, hy_freq1=hy_freq1, hy_w2=hy_w2,
                  hy_b2=hy_b2, hy_freq2=hy_freq2, hy_w3=hy_w3, hy_delta=hy_delta, hy_bias=hy_bias,
                  rw_mu=rw_mu, rw_w0=rw_w0, rw_w2=rw_w2, rw_a0=rw_a0, rw_a2=rw_a2, rw_g2=rw_g2,
                  rw_kk=rw_kk, rw_ka=rw_ka, rw_rk=rw_rk, rw_ln_g=rw_ln_g, rw_ln_b=rw_ln_b,
                  w_branch=w_branch, w_out=w_out, p_wq=p_wq, p_key1=p_key1, p_key2=p_key2,
                  p_u=p_u, p_v=p_v, final_norm=final_norm)
    return trunk(x_prompt, x_sample, state_hgrn, state_rwkv, c, c_ctx, params)
```

```python
import functools
import math
import jax
import jax.numpy as jnp
from jax import lax
import numpy as np
from jax.experimental import pallas as pl
from jax.experimental.pallas import tpu as pltpu

D_MODEL = 1024
DEPTH = 4
GRID_W = 64
BRANCH_WIDTH = 512
N_BRANCH = 3

A_DK = 128
A_DV = 128
A_HEADS = BRANCH_WIDTH // A_DV

HY_BANDS = 16
HY_EMB = 1 + 2 * HY_BANDS
HY_FFN = 64

C_HEAD_DIM = 64
C_HEADS = BRANCH_WIDTH // C_HEAD_DIM
C_WIDTH = BRANCH_WIDTH
C_DECAY_LORA = 64
C_AAA_LORA = 64
C_GATE_LORA = 128

A_COLS = 3 * A_HEADS * A_DK + A_HEADS * A_DV + BRANCH_WIDTH
B_COLS = 3 * BRANCH_WIDTH
C_COLS = 3 * C_WIDTH + 2 * C_DECAY_LORA + 2 * C_AAA_LORA + C_GATE_LORA
G_COLS = N_BRANCH * D_MODEL

P_HEADS = 8
P_NKEYS = 128
P_EXPERTS = P_NKEYS * P_NKEYS
P_TOPK = 16
P_KEY_DIM = 128

EPS = 1e-6
RWKV_GN_EPS = 64e-5

LANES = 128
ROW_TILE = 256
VMEM_LIMIT = 56 * 1024 * 1024

F32 = jnp.float32
BF16 = jnp.bfloat16


def _cparams(sem):
    return pltpu.CompilerParams(dimension_semantics=sem, vmem_limit_bytes=VMEM_LIMIT)


def _bdot(a, b):
    return jnp.dot(a.astype(BF16), b.astype(BF16), preferred_element_type=F32)


def _split_hi_lo(x):
    hi = x.astype(BF16)
    lo = (x - hi.astype(F32)).astype(BF16)
    return hi, lo


def _seg_of_tile(i, prompt_tiles, tiles_per_sample):
    return jnp.where(i < prompt_tiles, 0, 1 + (i - prompt_tiles) // tiles_per_sample)


def _norm_mod(x, gain, scale, shift):
    return x * lax.rsqrt(jnp.mean(x * x, axis=-1, keepdims=True) + EPS) * gain * (1.0 + scale) + shift


def _inproj_kernel(x_ref, gain_ref, scale_ref, shift_ref, w_ref, o_ref):
    h = _norm_mod(x_ref[...], gain_ref[...], scale_ref[0], shift_ref[0])
    o_ref[...] = jnp.dot(h.astype(BF16), w_ref[...], preferred_element_type=F32)


def norm_mod_matmul(x, gain, scale, shift, w_bf16, prompt_tiles, tiles_per_sample):
    n, d = x.shape
    cols = w_bf16.shape[1]
    seg = lambda i: (_seg_of_tile(i, prompt_tiles, tiles_per_sample), 0, 0)
    return pl.pallas_call(
        _inproj_kernel,
        grid=(n // ROW_TILE,),
        in_specs=[pl.BlockSpec((ROW_TILE, d), lambda i: (i, 0)),
                  pl.BlockSpec((1, d), lambda i: (0, 0)),
                  pl.BlockSpec((1, 1, d), seg),
                  pl.BlockSpec((1, 1, d), seg),
                  pl.BlockSpec((d, cols), lambda i: (0, 0))],
        out_specs=pl.BlockSpec((ROW_TILE, cols), lambda i: (i, 0)),
        out_shape=jax.ShapeDtypeStruct((n, cols), F32),
        compiler_params=_cparams(("parallel",)),
    )(x, gain.reshape(1, d), scale, shift, w_bf16)


HG_CHUNK = 128
HG_LEVELS = 7


def _hgrn_tables():
    c = HG_CHUNK
    out = np.zeros((2, HG_LEVELS + 2, c, c), np.float32)
    t = np.arange(c)
    for d in range(2):
        pos = t if d == 0 else c - 1 - t
        pt, pu = pos[:, None], pos[None, :]
        for lv in range(HG_LEVELS):
            m = c >> (lv + 1)
            same = (pt // m) == (pu // m)
            query = ((pt // m) % 2) == 1
            out[d, lv] = same & np.where(query, pu <= pt, pu > pt)
        out[d, HG_LEVELS] = pu <= pt
        out[d, HG_LEVELS + 1] = pu > pt
    return out.reshape(2, (HG_LEVELS + 2) * c, c)


def _hgrn_kernel(q_ref, fz_ref, v_ref, lb_ref, tab_ref, s0_ref, o_ref, sfin_ref, st_scr, *, n_chunks):
    c = HG_CHUNK
    d = pl.program_id(1)
    ci = pl.program_id(2)

    @pl.when(ci == 0)
    def _():
        for h in range(A_HEADS):
            st_scr[h] = s0_ref[0, 0, h].T

    row = lax.broadcasted_iota(jnp.int32, (c, LANES), 0)
    pos_r = row + d * (c - 1 - 2 * row)
    ti = lax.broadcasted_iota(jnp.int32, (c, c), 0)
    si = lax.broadcasted_iota(jnp.int32, (c, c), 1)
    pos_t = ti + d * (c - 1 - 2 * ti)
    pos_s = si + d * (c - 1 - 2 * si)
    tab = tab_ref[0]

    for h in range(A_HEADS):
        hs = pl.ds(h * LANES, LANES)
        qz = q_ref[:, hs]
        q = qz * jax.nn.sigmoid(qz)
        lb = lb_ref[0, :, hs]
        f = lb + (1.0 - lb) * jax.nn.sigmoid(fz_ref[:, hs])
        kd = 1.0 - f
        logf = jnp.log(f)
        v = v_ref[:, hs]

        lhi, llo = _split_hi_lo(logf)
        args = (jnp.dot(tab, lhi, preferred_element_type=F32)
                + jnp.dot(tab, llo, preferred_element_type=F32))
        e_all = jnp.exp(args)

        attn = jnp.where(ti == si, jnp.sum(q * kd, axis=-1, keepdims=True), 0.0)
        for lv in range(HG_LEVELS):
            sh = HG_LEVELS - 1 - lv
            e = e_all[lv * c:(lv + 1) * c]
            is_q = ((pos_r >> sh) & 1) == 1
            qs = jnp.where(is_q, q * e, 0.0).astype(BF16)
            ks = jnp.where(is_q, 0.0, kd * e).astype(BF16)
            sc = lax.dot_general(qs, ks, (((1,), (1,)), ((), ())), preferred_element_type=F32)
            attn = attn + jnp.where((pos_t >> (sh + 1)) == (pos_s >> (sh + 1)), sc, 0.0)

        e_cum = e_all[HG_LEVELS * c:(HG_LEVELS + 1) * c]
        e_rev = e_all[(HG_LEVELS + 1) * c:]
        st = st_scr[h]
        o = _bdot(attn, v) + lax.dot_general((q * e_cum).astype(BF16), st.astype(BF16),
                                             (((1,), (1,)), ((), ())), preferred_element_type=F32)
        o_ref[0, 0, :, hs] = o
        dec = jnp.exp(jnp.sum(logf, axis=0, keepdims=True))
        st_new = st * dec + _bdot(v.T, kd * e_rev)
        st_scr[h] = st_new

        @pl.when(ci == n_chunks - 1)
        def _():
            sfin_ref[0, 0, h] = st_new.T


def hgrn2_scan(pa, lb, s0, row_off, batch, seqlen):
    c = HG_CHUNK
    n_chunks = seqlen // c
    base = row_off // c
    hw = A_HEADS * A_DK
    tab = jnp.asarray(_hgrn_tables(), BF16)

    def rows(b, ci, d):
        return base + b * n_chunks + ci + d * (n_chunks - 1 - 2 * ci)

    def orow(b, ci, d):
        return b * n_chunks + ci + d * (n_chunks - 1 - 2 * ci)

    return pl.pallas_call(
        functools.partial(_hgrn_kernel, n_chunks=n_chunks),
        grid=(batch, 2, n_chunks),
        in_specs=[pl.BlockSpec((c, hw), lambda b, d, ci: (rows(b, ci, d), 0)),
                  pl.BlockSpec((c, hw), lambda b, d, ci: (rows(b, ci, d), 1 + d)),
                  pl.BlockSpec((c, hw), lambda b, d, ci: (rows(b, ci, d), 3)),
                  pl.BlockSpec((1, 1, hw), lambda b, d, ci: (d, 0, 0)),
                  pl.BlockSpec((1, (HG_LEVELS + 2) * c, c), lambda b, d, ci: (d, 0, 0)),
                  pl.BlockSpec((1, 1, A_HEADS, A_DK, A_DV), lambda b, d, ci: (b, d, 0, 0, 0))],
        out_specs=[pl.BlockSpec((1, 1, c, hw), lambda b, d, ci: (d, 0, orow(b, ci, d), 0)),
                   pl.BlockSpec((1, 1, A_HEADS, A_DK, A_DV), lambda b, d, ci: (b, d, 0, 0, 0))],
        out_shape=[jax.ShapeDtypeStruct((2, 1, batch * seqlen, A_HEADS * A_DV), F32),
                   jax.ShapeDtypeStruct((batch, 2, A_HEADS, A_DK, A_DV), F32)],
        scratch_shapes=[pltpu.VMEM((A_HEADS, A_DV, A_DK), F32)],
        compiler_params=_cparams(("parallel", "parallel", "arbitrary")),
    )(pa, pa, pa, lb, tab, s0)


def _dft_table_kernel(fc_ref, fs_ref, gc_ref, gs_ref, *, length, tile):
    n2 = 2 * length
    a = lax.broadcasted_iota(jnp.int32, (tile, length), 0) + pl.program_id(0) * tile
    b = lax.broadcasted_iota(jnp.int32, (tile, length), 1)
    ang = ((a * b) & (n2 - 1)).astype(F32) * (2.0 * math.pi / n2)
    co, si = jnp.cos(ang), jnp.sin(ang)
    alt_b = (1 - 2 * (b & 1)).astype(F32)
    alt_a = (1 - 2 * (a & 1)).astype(F32)
    fc_ref[...] = co.astype(BF16)
    fs_ref[...] = jnp.where(a == 0, alt_b, -si).astype(BF16)
    wk = jnp.where(b == 0, 1.0 / n2, 2.0 / n2)
    gc_ref[...] = (wk * co).astype(BF16)
    gs_ref[...] = jnp.where(b == 0, alt_a * (1.0 / n2), -wk * si).astype(BF16)


def dft_tables(length):
    tile = min(length, 256)
    spec = pl.BlockSpec((tile, length), lambda i: (i, 0))
    shp = jax.ShapeDtypeStruct((length, length), BF16)
    return pl.pallas_call(
        functools.partial(_dft_table_kernel, length=length, tile=tile),
        grid=(length // tile,),
        out_specs=[spec] * 4, out_shape=[shp] * 4,
        compiler_params=_cparams(("parallel",)),
    )()


def _dft_fwd_kernel(fc_ref, fs_ref, z_ref, zr_ref, zi_ref):
    z = z_ref[0].astype(BF16)
    zr_ref[0] = jnp.dot(fc_ref[...], z, preferred_element_type=F32)
    zi_ref[0] = jnp.dot(fs_ref[...], z, preferred_element_type=F32)


def dft_forward(fc, fs, z):
    batch, length, ch = z.shape
    tile = min(length, 512)
    return pl.pallas_call(
        _dft_fwd_kernel,
        grid=(length // tile, batch),
        in_specs=[pl.BlockSpec((tile, length), lambda i, b: (i, 0)),
                  pl.BlockSpec((tile, length), lambda i, b: (i, 0)),
                  pl.BlockSpec((1, length, ch), lambda i, b: (b, 0, 0))],
        out_specs=[pl.BlockSpec((1, tile, ch), lambda i, b: (b, i, 0))] * 2,
        out_shape=[jax.ShapeDtypeStruct((batch, length, ch), F32)] * 2,
        compiler_params=_cparams(("parallel", "arbitrary")),
    )(fc, fs, z)


def _spec_mul_kernel(zr_ref, zi_ref, kr_ref, ki_ref, yr_ref, yi_ref, *, tile):
    row = lax.broadcasted_iota(jnp.int32, (tile, 1), 0) + pl.program_id(1) * tile
    sgn = (1 - 2 * (row & 1)).astype(F32)
    kr = kr_ref[0] + sgn * kr_ref[1]
    ki = ki_ref[0] + sgn * ki_ref[1]
    zr, zi = zr_ref[0], zi_ref[0]
    packed = row == 0
    yr_ref[0] = jnp.where(packed, zr * kr, zr * kr - zi * ki).astype(BF16)
    yi_ref[0] = jnp.where(packed, zi * ki, zr * ki + zi * kr).astype(BF16)


def spectrum_multiply(zr, zi, kr, ki):
    batch, length, ch = zr.shape
    tile = min(length, 512)
    zspec = pl.BlockSpec((1, tile, ch), lambda b, i: (b, i, 0))
    kspec = pl.BlockSpec((2, tile, ch), lambda b, i: (0, i, 0))
    return pl.pallas_call(
        functools.partial(_spec_mul_kernel, tile=tile),
        grid=(batch, length // tile),
        in_specs=[zspec, zspec, kspec, kspec],
        out_specs=[zspec, zspec],
        out_shape=[jax.ShapeDtypeStruct((batch, length, ch), BF16)] * 2,
        compiler_params=_cparams(("parallel", "parallel")),
    )(zr, zi, kr, ki)


def _dft_inv_kernel(gc_ref, gs_ref, yr_ref, yi_ref, x0_ref, z_ref, bias_ref, o_ref):
    y = (jnp.dot(gc_ref[...], yr_ref[0], preferred_element_type=F32)
         + jnp.dot(gs_ref[...], yi_ref[0], preferred_element_type=F32))
    o_ref[0] = x0_ref[0] * (y + z_ref[0] * bias_ref[...])


def dft_inverse_gate(gc, gs, yr, yi, x0, z, bias):
    batch, length, ch = yr.shape
    tile = min(length, 512)
    tspec = pl.BlockSpec((1, tile, ch), lambda i, b: (b, i, 0))
    return pl.pallas_call(
        _dft_inv_kernel,
        grid=(length // tile, batch),
        in_specs=[pl.BlockSpec((tile, length), lambda i, b: (i, 0)),
                  pl.BlockSpec((tile, length), lambda i, b: (i, 0)),
                  pl.BlockSpec((1, length, ch), lambda i, b: (b, 0, 0)),
                  pl.BlockSpec((1, length, ch), lambda i, b: (b, 0, 0)),
                  tspec, tspec,
                  pl.BlockSpec((1, ch), lambda i, b: (0, 0))],
        out_specs=tspec,
        out_shape=jax.ShapeDtypeStruct((batch, length, ch), F32),
        compiler_params=_cparams(("parallel", "arbitrary")),
    )(gc, gs, yr, yi, x0, z, bias.reshape(1, ch))


def shift_prev(x):
    return jnp.pad(x, ((0, 0), (1, 0), (0, 0)))[:, :-1]


def shift_next(x):
    return jnp.pad(x, ((0, 0), (0, 1), (0, 0)))[:, 1:]


def hyena_filter(length, lp):
    hp = lax.Precision.HIGHEST
    t = jnp.linspace(0.0, 1.0, length, dtype=F32)[:, None]
    bands = jnp.linspace(1e-4, HY_BANDS - 1, HY_BANDS, dtype=F32)[None, :]
    ang = (2 * math.pi / length) * jnp.arange(length, dtype=F32)[:, None] * bands
    z = jnp.concatenate([t, jnp.cos(ang), -jnp.sin(ang)], axis=-1)
    hf = jnp.sin(lp['hy_freq1'] * (jnp.dot(z, lp['hy_w1'], precision=hp) + lp['hy_b1']))
    hf = jnp.sin(lp['hy_freq2'] * (jnp.dot(hf, lp['hy_w2'], precision=hp) + lp['hy_b2']))
    hf = jnp.dot(hf, lp['hy_w3'], precision=hp).reshape(length, 2, BRANCH_WIDTH)
    hf = hf * jnp.exp(-t * jnp.abs(lp['hy_delta']))[:, None, :]
    lo = hf[:, 0]
    hi = jnp.concatenate([jnp.zeros((1, BRANCH_WIDTH), F32), hf[:0:-1, 1]], axis=0)
    norm = jnp.sum(jnp.abs(lo), axis=0, keepdims=True) + jnp.sum(jnp.abs(hi), axis=0, keepdims=True)
    return jnp.stack([lo, hi]) / norm


def hyena_branch(pb, lp, tables):
    batch, length, _ = pb.shape
    fc, fs, gc, gs = tables
    cw = lp['hy_conv_w']
    u = shift_prev(pb) * cw[0] + pb * cw[1] + shift_next(pb) * cw[2] + lp['hy_conv_b']
    x0, x1, v = jnp.split(u, 3, axis=-1)
    z = v * x1
    kr, ki = dft_forward(fc, fs, hyena_filter(length, lp))
    zr, zi = dft_forward(fc, fs, z)
    yr, yi = spectrum_multiply(zr, zi, kr, ki)
    return dft_inverse_gate(gc, gs, yr, yi, x0, z, lp['hy_bias'])


RW_GROUP_BATCH = 4
RW_VROWS = C_HEAD_DIM // 2
RW_TCHUNK = 32
RW_STACK = 6


def _head_sum(y, bd):
    hi, lo = _split_hi_lo(y)
    return jnp.dot(hi, bd, preferred_element_type=F32) + jnp.dot(lo, bd, preferred_element_type=F32)


def _split3(x):
    hi = x.astype(BF16)
    r1 = x - hi.astype(F32)
    mid = r1.astype(BF16)
    return hi, mid, (r1 - mid.astype(F32)).astype(BF16)


def _reverse_lanes(x, flip):
    hi, mid, lo = _split3(x)
    dot = lambda p: jnp.dot(p, flip, preferred_element_type=F32)
    return (dot(hi) + dot(mid)) + dot(lo)


def _exchange(n):
    return jnp.asarray(np.eye(n)[::-1], BF16)


def _head_ones():
    h = np.arange(C_WIDTH) % C_HEADS
    return jnp.asarray(h[:, None] == h[None, :], BF16)


RW_PERM = np.arange(C_WIDTH).reshape(C_HEADS, C_HEAD_DIM).T.reshape(-1)


def rwkv_permuted_params(lp, w_c):
    p = RW_PERM
    cols = np.concatenate([p, C_WIDTH + p, 2 * C_WIDTH + p, np.arange(3 * C_WIDTH, C_COLS)])
    return dict(w_c=w_c[:, cols], rw_mu=lp['rw_mu'][cols], rw_kk=lp['rw_kk'][p], rw_ka=lp['rw_ka'][p],
                rw_rk=lp['rw_rk'].reshape(-1)[p], rw_w0=lp['rw_w0'][:, p], rw_w2=lp['rw_w2'][:, :, p],
                rw_a0=lp['rw_a0'][:, p], rw_a2=lp['rw_a2'][:, :, p], rw_g2=lp['rw_g2'][:, p],
                rw_ln_g=lp['rw_ln_g'][p], rw_ln_b=lp['rw_ln_b'][p], w_branch_c=lp['w_branch'][2][p])


def _rwkv_pre_kernel(prv_ref, cur_ref, nxt_ref, mu_ref, kks_ref, ka_ref, rk_ref, w0_ref, w2_ref, a0_ref,
                     a2_ref, g2_ref, bd_ref, flip_ref, *out_refs, prompt_tiles, tiles_per_prompt, tiles_per_sample,
                     prompt_len, sample_len):
    i = pl.program_id(0)
    tm = ROW_TILE
    latent = i >= prompt_tiles
    tile_in_seq = jnp.where(latent, (i - prompt_tiles) % tiles_per_sample, i % tiles_per_prompt)
    seq_len = jnp.where(latent, sample_len, prompt_len)
    row = lax.broadcasted_iota(jnp.int32, (tm, 1), 0)
    pos = tile_in_seq * tm + row
    col = pos % GRID_W
    one = lambda c: jnp.where(c, 1.0, 0.0)
    m_left = jnp.where(latent, one(col != 0), one(pos != 0))
    m_right = jnp.where(latent, one(col != GRID_W - 1), one(pos != seq_len - 1))
    m_up = one(pos >= GRID_W)
    m_down = one(pos < seq_len - GRID_W)
    c_lr = jnp.where(latent, 0.25, 0.5)
    c_ud = jnp.where(latent, 0.25, 0.0)

    cur = cur_ref[...]
    prv = prv_ref[...]
    nxt = nxt_ref[...]
    left = jnp.where(row == 0, prv[tm - 1:tm], pltpu.roll(cur, 1, axis=0))
    right = jnp.where(row == tm - 1, nxt[0:1], pltpu.roll(cur, tm - 1, axis=0))
    up = jnp.concatenate([prv[tm - GRID_W:], cur[:tm - GRID_W]], axis=0)
    down = jnp.concatenate([cur[GRID_W:], nxt[:GRID_W]], axis=0)
    nm = c_lr * (left * m_left + right * m_right) + c_ud * (up * m_up + down * m_down)
    x = cur + (nm - cur) * mu_ref[...]

    cw = C_WIDTH
    r, k, v = x[:, :cw], x[:, cw:2 * cw], x[:, 2 * cw:3 * cw]
    o = 3 * cw
    wd = x[:, o:o + 2 * C_DECAY_LORA]
    ad = x[:, o + 2 * C_DECAY_LORA:o + 2 * C_DECAY_LORA + 2 * C_AAA_LORA]
    gd = x[:, o + 2 * C_DECAY_LORA + 2 * C_AAA_LORA:]
    bd = bd_ref[...]

    kk = k * kks_ref[...]
    kkn = kk * lax.rsqrt(_head_sum(kk * kk, bd) + 1e-12)
    ptf_o, ptb_o, g_o, bonus_o = out_refs
    flip = flip_ref[...]
    for idx, arr in enumerate((r, v, -kkn)):
        at = arr.T
        ptf_o[idx] = at
        ptb_o[idx] = _reverse_lanes(at, flip)
    g_o[...] = _bdot(jax.nn.sigmoid(gd), g2_ref[...])
    bonus_o[...] = _head_sum(r * k * rk_ref[...], bd) * v
    for d, pt_o in enumerate((ptf_o, ptb_o)):
        z = -(w0_ref[d] + _bdot(jnp.tanh(wd[:, d * C_DECAY_LORA:(d + 1) * C_DECAY_LORA]), w2_ref[d]))
        softplus = jnp.maximum(z, 0.0) + jnp.log(1.0 + jnp.exp(-jnp.abs(z)))
        ag = jax.nn.sigmoid(a0_ref[d] + _bdot(ad[:, d * C_AAA_LORA:(d + 1) * C_AAA_LORA], a2_ref[d]))
        outs = (jnp.exp(-jnp.exp(-softplus - 0.5)), k * (1.0 + (ag - 1.0) * ka_ref[...]), kkn * ag)
        for idx, arr in enumerate(outs):
            at = arr.T
            pt_o[3 + idx] = at if d == 0 else _reverse_lanes(at, flip)


def rwkv_pre(pc, rp, prompt_tiles, tiles_per_prompt, tiles_per_sample, prompt_len, sample_len):
    n, cc = pc.shape
    cw = C_WIDTH
    nt = n // ROW_TILE
    vec = lambda a: a.reshape(1, -1)
    const2 = lambda shp: pl.BlockSpec(shp, lambda i: (0, 0))
    const3 = lambda shp: pl.BlockSpec(shp, lambda i: (0, 0, 0))
    ospec = pl.BlockSpec((ROW_TILE, cw), lambda i: (i, 0))

    def mirrored(i):
        per = jnp.where(i < prompt_tiles, tiles_per_prompt, tiles_per_sample)
        rel = jnp.where(i < prompt_tiles, i, i - prompt_tiles)
        return i - 2 * (rel % per) + per - 1

    kern = functools.partial(_rwkv_pre_kernel, prompt_tiles=prompt_tiles, tiles_per_prompt=tiles_per_prompt,
                             tiles_per_sample=tiles_per_sample, prompt_len=prompt_len, sample_len=sample_len)
    return pl.pallas_call(
        kern,
        grid=(nt,),
        in_specs=[pl.BlockSpec((ROW_TILE, cc), lambda i: (jnp.maximum(i - 1, 0), 0)),
                  pl.BlockSpec((ROW_TILE, cc), lambda i: (i, 0)),
                  pl.BlockSpec((ROW_TILE, cc), lambda i: (jnp.minimum(i + 1, nt - 1), 0)),
                  const2((1, cc)), const2((1, cw)), const2((1, cw)), const2((1, cw)),
                  const3((2, 1, cw)), const3((2, C_DECAY_LORA, cw)),
                  const3((2, 1, cw)), const3((2, C_AAA_LORA, cw)),
                  const2((C_GATE_LORA, cw)), const2((cw, cw)), const2((ROW_TILE, ROW_TILE))],
        out_specs=[pl.BlockSpec((RW_STACK, cw, ROW_TILE), lambda i: (0, 0, i)),
                   pl.BlockSpec((RW_STACK, cw, ROW_TILE), lambda i: (0, 0, mirrored(i))), ospec, ospec],
        out_shape=[jax.ShapeDtypeStruct((RW_STACK, cw, n), F32), jax.ShapeDtypeStruct((RW_STACK, cw, n), F32),
                   jax.ShapeDtypeStruct((n, cw), F32), jax.ShapeDtypeStruct((n, cw), F32)],
        compiler_params=_cparams(("parallel",)),
    )(pc, pc, pc, vec(rp['rw_mu']), vec(rp['rw_kk']), vec(rp['rw_ka']), vec(rp['rw_rk']),
      rp['rw_w0'][:, None, :], rp['rw_w2'].astype(BF16), rp['rw_a0'][:, None, :], rp['rw_a2'].astype(BF16),
      rp['rw_g2'].astype(BF16), _head_ones(), _exchange(ROW_TILE))


RW_RELAYOUT_T = 128
RW_KQ = 4
RW_KROWS = (0, 3, 4, 2, 5)


def _rwkv_lanes_kernel(f0, f1, f2, f3, b0, b1, b2, b3, kl_ref, vl_ref):
    q = pl.program_id(2)
    stacks = ((f0, f1, f2, f3), (b0, b1, b2, b3))
    nk = C_HEAD_DIM // RW_KQ
    for kl in range(nk):
        r0 = pl.multiple_of((q * nk + kl) * C_HEADS, C_HEADS)
        for ai, src in enumerate(RW_KROWS):
            base = [st[b][src, pl.ds(r0, C_HEADS), :] for st in stacks for b in range(RW_GROUP_BATCH)]
            kl_ref[0, ai, kl] = jnp.concatenate(base + base, axis=0).T
    nv = RW_VROWS // RW_KQ
    for vi in range(nv):
        pieces = []
        for vh in range(2):
            r0 = pl.multiple_of((vh * RW_VROWS + q * nv + vi) * C_HEADS, C_HEADS)
            pieces += [st[b][1, pl.ds(r0, C_HEADS), :] for st in stacks for b in range(RW_GROUP_BATCH)]
        vl_ref[0, vi] = jnp.concatenate(pieces, axis=0).T


def rwkv_to_lanes(ptf, ptb, off, batch, seqlen):
    g = batch // RW_GROUP_BATCH
    tt = RW_RELAYOUT_T
    nj = seqlen // tt
    col0 = off // tt
    nk, nv = C_HEAD_DIM // RW_KQ, RW_VROWS // RW_KQ

    def src(b):
        return pl.BlockSpec((RW_STACK, C_WIDTH, tt),
                            lambda gi, j, q: (0, 0, col0 + (gi * RW_GROUP_BATCH + b) * nj + j))

    specs = [src(b) for b in range(RW_GROUP_BATCH)]
    return pl.pallas_call(
        _rwkv_lanes_kernel,
        grid=(g, nj, RW_KQ),
        in_specs=specs + specs,
        out_specs=[pl.BlockSpec((1, 5, nk, tt, LANES), lambda gi, j, q: (gi, 0, q, j, 0)),
                   pl.BlockSpec((1, nv, tt, LANES), lambda gi, j, q: (gi, q, j, 0))],
        out_shape=[jax.ShapeDtypeStruct((g, 5, C_HEAD_DIM, seqlen, LANES), F32),
                   jax.ShapeDtypeStruct((g, RW_VROWS, seqlen, LANES), F32)],
        compiler_params=_cparams(("parallel", "parallel", "arbitrary")),
    )(ptf, ptf, ptf, ptf, ptb, ptb, ptb, ptb)


def _rwkv_scan_kernel(k_ref, v_ref, s0_ref, y_ref, sfin_ref, s_scr, row_scr, v_scr, y_scr, *, n_tc):
    tc = pl.program_id(1)

    @pl.when(tc == 0)
    def _():
        s_scr[...] = s0_ref[0]

    nk = C_HEAD_DIM
    R, W, KD, A, B = range(5)
    for t0 in range(0, RW_TCHUNK, 8):
        for a in range(5):
            for k0 in range(0, nk, 8):
                row_scr[pl.ds(t0, 8), a, pl.ds(k0, 8), :] = jnp.swapaxes(
                    k_ref[0, a, pl.ds(k0, 8), pl.ds(t0, 8), :], 0, 1)
        for r0 in range(0, RW_VROWS, 8):
            v_scr[pl.ds(t0, 8), pl.ds(r0, 8), :] = jnp.swapaxes(v_ref[0, pl.ds(r0, 8), pl.ds(t0, 8), :], 0, 1)

    def step(t, carry):
        parts = [jnp.zeros((RW_VROWS, LANES), F32) for _ in range(4)]
        for kk in range(nk):
            parts[kk % 4] = parts[kk % 4] + s_scr[kk] * row_scr[t, A, pl.ds(kk, 1), :]
        sa = (parts[0] + parts[1]) + (parts[2] + parts[3])
        v_t = v_scr[t]
        ys = [jnp.zeros((RW_VROWS, LANES), F32) for _ in range(4)]
        for kk in range(nk):
            s_new = (s_scr[kk] * row_scr[t, W, pl.ds(kk, 1), :] + sa * row_scr[t, B, pl.ds(kk, 1), :]
                     + v_t * row_scr[t, KD, pl.ds(kk, 1), :])
            s_scr[kk] = s_new
            ys[kk % 4] = ys[kk % 4] + s_new * row_scr[t, R, pl.ds(kk, 1), :]
        y_scr[t] = (ys[0] + ys[1]) + (ys[2] + ys[3])
        return carry

    lax.fori_loop(0, RW_TCHUNK, step, 0)
    for t0 in range(0, RW_TCHUNK, 8):
        for r0 in range(0, RW_VROWS, 8):
            y_ref[0, pl.ds(r0, 8), pl.ds(t0, 8), :] = jnp.swapaxes(y_scr[pl.ds(t0, 8), pl.ds(r0, 8), :], 0, 1)

    @pl.when(tc == n_tc - 1)
    def _():
        sfin_ref[0] = s_scr[...]


def rwkv7_scan_lanes(kl, vl, s0):
    g, t = vl.shape[0], vl.shape[2]
    n_tc = t // RW_TCHUNK
    kspec = pl.BlockSpec((1, 5, C_HEAD_DIM, RW_TCHUNK, LANES), lambda gi, ti: (gi, 0, 0, ti, 0))
    vspec = pl.BlockSpec((1, RW_VROWS, RW_TCHUNK, LANES), lambda gi, ti: (gi, 0, ti, 0))
    sspec = pl.BlockSpec((1, C_HEAD_DIM, RW_VROWS, LANES), lambda gi, ti: (gi, 0, 0, 0))
    return pl.pallas_call(
        functools.partial(_rwkv_scan_kernel, n_tc=n_tc),
        grid=(g, n_tc),
        in_specs=[kspec, vspec, sspec],
        out_specs=[vspec, sspec],
        out_shape=[jax.ShapeDtypeStruct((g, RW_VROWS, t, LANES), F32),
                   jax.ShapeDtypeStruct((g, C_HEAD_DIM, RW_VROWS, LANES), F32)],
        scratch_shapes=[pltpu.VMEM((C_HEAD_DIM, RW_VROWS, LANES), F32),
                        pltpu.VMEM((RW_TCHUNK, 5, C_HEAD_DIM, LANES), F32),
                        pltpu.VMEM((RW_TCHUNK, RW_VROWS, LANES), F32),
                        pltpu.VMEM((RW_TCHUNK, RW_VROWS, LANES), F32)],
        compiler_params=_cparams(("parallel", "arbitrary")),
    )(kl, vl, s0)


def _rwkv_unlanes_kernel(yf_ref, yb_ref, flip_ref, o_ref):
    half = LANES // 2
    flip = flip_ref[...]
    for vl in range(RW_VROWS):
        tf = yf_ref[0, vl].T
        tb = _reverse_lanes(yb_ref[0, vl].T, flip)
        for vh in range(2):
            for b in range(RW_GROUP_BATCH):
                rf = vh * half + b * C_HEADS
                rb = rf + half // 2
                o_ref[0, b, pl.ds((vh * RW_VROWS + vl) * C_HEADS, C_HEADS), :] = (
                    tf[rf:rf + C_HEADS] + tb[rb:rb + C_HEADS])


def rwkv_from_lanes(y, batch, seqlen):
    g = batch // RW_GROUP_BATCH
    tt = RW_RELAYOUT_T
    nj = seqlen // tt
    out = pl.pallas_call(
        _rwkv_unlanes_kernel,
        grid=(g, nj),
        in_specs=[pl.BlockSpec((1, RW_VROWS, tt, LANES), lambda gi, j: (gi, 0, j, 0)),
                  pl.BlockSpec((1, RW_VROWS, tt, LANES), lambda gi, j: (gi, 0, nj - 1 - j, 0)),
                  pl.BlockSpec((tt, tt), lambda gi, j: (0, 0))],
        out_specs=pl.BlockSpec((1, RW_GROUP_BATCH, C_WIDTH, tt), lambda gi, j: (gi, 0, 0, j)),
        out_shape=jax.ShapeDtypeStruct((g, RW_GROUP_BATCH, C_WIDTH, seqlen), F32),
        compiler_params=_cparams(("parallel", "parallel")),
    )(y, y, _exchange(tt))
    return out.reshape(batch, C_WIDTH, seqlen)


def _state_to_lanes(s):
    bsz = s.shape[0]
    g = bsz // RW_GROUP_BATCH
    s = s.reshape(g, RW_GROUP_BATCH, 2, C_HEADS, 2, RW_VROWS, C_HEAD_DIM)
    return jnp.transpose(s, (0, 6, 5, 4, 2, 1, 3)).reshape(g, C_HEAD_DIM, RW_VROWS, LANES)


def _state_from_lanes(s):
    g = s.shape[0]
    s = s.reshape(g, C_HEAD_DIM, RW_VROWS, 2, 2, RW_GROUP_BATCH, C_HEADS)
    s = jnp.transpose(s, (0, 5, 4, 6, 3, 2, 1))
    return s.reshape(g * RW_GROUP_BATCH, 2, C_HEADS, C_HEAD_DIM, C_HEAD_DIM)


def rwkv7_scan_segment(ptf, ptb, s0, off, batch, seqlen):
    kl, vl = rwkv_to_lanes(ptf, ptb, off, batch, seqlen)
    y, sfin = rwkv7_scan_lanes(kl, vl, _state_to_lanes(s0))
    return rwkv_from_lanes(y, batch, seqlen), _state_from_lanes(sfin)


def _merge_kernel(x_ref, oa_ref, ga_ref, gn_ref, yb_ref, yrp_ref, yrs_ref, bonus_ref, rg_ref, lng_ref, lnb_ref,
                  bd_ref, pg_ref, gate_ref, wb_ref, wo_ref, o_ref, *, prompt_tiles):
    oa = oa_ref[0, 0] + oa_ref[1, 0]
    gz = ga_ref[...]
    gn = gn_ref[...]
    parts = []
    for h in range(A_HEADS):
        oh = oa[:, h * A_DV:(h + 1) * A_DV]
        parts.append(oh * lax.rsqrt(jnp.mean(oh * oh, axis=-1, keepdims=True) + EPS) * gn)
    ya = jnp.concatenate(parts, axis=-1) * (gz * jax.nn.sigmoid(gz))
    bd = bd_ref[...]
    yr = jnp.where(pl.program_id(0) < prompt_tiles, yrp_ref[0], yrs_ref[0]).T
    inv = 1.0 / C_HEAD_DIM
    yr = yr - _head_sum(yr, bd) * inv
    var = _head_sum(yr * yr, bd) * inv
    yc = (yr * lax.rsqrt(var + RWKV_GN_EPS) * lng_ref[...] + lnb_ref[...] + bonus_ref[...]) * rg_ref[...]
    pg = pg_ref[...]
    d = D_MODEL
    merged = (jax.nn.sigmoid(pg[:, :d]) * _bdot(ya, wb_ref[0])
              + jax.nn.sigmoid(pg[:, d:2 * d]) * _bdot(yb_ref[...], wb_ref[1])
              + jax.nn.sigmoid(pg[:, 2 * d:]) * _bdot(yc, wb_ref[2]))
    o_ref[...] = x_ref[...] + gate_ref[0] * _bdot(merged, wo_ref[...])


def merge_out(x, o_hgrn, pa, hgrn_norm, yb, yr_prompt, yr_sample, bonus, rw_gate, ln_g, ln_b, pg, gate, wb_bf16,
              wo_bf16, prompt_tiles, tiles_per_sample):
    n, d = x.shape
    bw = BRANCH_WIDTH
    row = lambda i: (i, 0)
    seg = lambda i: (_seg_of_tile(i, prompt_tiles, tiles_per_sample), 0, 0)
    rspec = pl.BlockSpec((ROW_TILE, bw), row)
    vspec = pl.BlockSpec((1, bw), lambda i: (0, 0))
    tiles_per_prompt = yr_prompt.shape[2] // ROW_TILE
    last_p = prompt_tiles - 1

    def prompt_blk(i):
        ic = jnp.minimum(i, last_p)
        return (ic // tiles_per_prompt, 0, ic % tiles_per_prompt)

    def sample_blk(i):
        ic = jnp.maximum(i - prompt_tiles, 0)
        return (ic // tiles_per_sample, 0, ic % tiles_per_sample)

    return pl.pallas_call(
        functools.partial(_merge_kernel, prompt_tiles=prompt_tiles),
        grid=(n // ROW_TILE,),
        in_specs=[pl.BlockSpec((ROW_TILE, d), row),
                  pl.BlockSpec((2, 1, ROW_TILE, bw), lambda i: (0, 0, i, 0)),
                  pl.BlockSpec((ROW_TILE, bw), lambda i: (i, (A_COLS - bw) // bw)),
                  pl.BlockSpec((1, A_DV), lambda i: (0, 0)),
                  rspec,
                  pl.BlockSpec((1, bw, ROW_TILE), prompt_blk),
                  pl.BlockSpec((1, bw, ROW_TILE), sample_blk),
                  rspec, rspec, vspec, vspec,
                  pl.BlockSpec((bw, bw), lambda i: (0, 0)),
                  pl.BlockSpec((ROW_TILE, G_COLS), row),
                  pl.BlockSpec((1, 1, d), seg),
                  pl.BlockSpec((N_BRANCH, bw, d), lambda i: (0, 0, 0)),
                  pl.BlockSpec((d, d), lambda i: (0, 0))],
        out_specs=pl.BlockSpec((ROW_TILE, d), row),
        out_shape=jax.ShapeDtypeStruct((n, d), F32),
        compiler_params=_cparams(("parallel",)),
    )(x, o_hgrn, pa, hgrn_norm.reshape(1, A_DV), yb, yr_prompt, yr_sample, bonus, rw_gate, ln_g.reshape(1, bw),
      ln_b.reshape(1, bw), _head_ones(), pg, gate, wb_bf16, wo_bf16)


PEER_SEL_TILE = 256
PEER_TOK_TILE = 512
PEER_EXP_TILE = 1024
PEER_RANKS = P_TOPK + 1


def _nt_x3_kernel(a_ref, b_ref, hi_ref, lo_ref):
    ahi, alo = _split_hi_lo(a_ref[...])
    bhi, blo = _split_hi_lo(b_ref[...])
    nt = lambda p, q: lax.dot_general(p, q, (((1,), (1,)), ((), ())), preferred_element_type=F32)
    m = nt(ahi, bhi) + (nt(ahi, blo) + nt(alo, bhi))
    hi, lo = _split_hi_lo(m)
    hi_ref[...] = hi
    lo_ref[...] = lo


def peer_score_matrix(p_wq, p_key1, p_key2):
    d = p_wq.shape[0]
    eye = jnp.eye(P_HEADS, dtype=F32)
    k1 = jnp.einsum('nd,hg->hngd', p_key1, eye)
    k2 = jnp.einsum('nd,hg->hngd', p_key2, eye)
    zeros = jnp.zeros_like(k1)
    kbig = jnp.stack([jnp.stack([k1, zeros], axis=3), jnp.stack([zeros, k2], axis=3)])
    kbig = kbig.reshape(2 * P_HEADS * P_NKEYS, P_HEADS * P_KEY_DIM)
    rows = kbig.shape[0]
    tile = 256
    return pl.pallas_call(
        _nt_x3_kernel,
        grid=(rows // tile,),
        in_specs=[pl.BlockSpec((tile, P_HEADS * P_KEY_DIM), lambda i: (i, 0)),
                  pl.BlockSpec((d, P_HEADS * P_KEY_DIM), lambda i: (0, 0))],
        out_specs=[pl.BlockSpec((tile, d), lambda i: (i, 0))] * 2,
        out_shape=[jax.ShapeDtypeStruct((rows, d), BF16)] * 2,
        compiler_params=_cparams(("parallel",)),
    )(kbig, p_wq)


def _peer_scores_kernel(x_ref, gain_ref, scale_ref, shift_ref, mhi_ref, mlo_ref, ht_ref, s_ref):
    h = _norm_mod(x_ref[...], gain_ref[...], scale_ref[0], shift_ref[0])
    ht = h.T
    hhi, hlo = _split_hi_lo(ht)
    ht_ref[...] = hhi
    mhi = mhi_ref[...]
    s_ref[...] = (jnp.dot(mhi, hhi, preferred_element_type=F32)
                  + (jnp.dot(mhi, hlo, preferred_element_type=F32)
                     + jnp.dot(mlo_ref[...], hhi, preferred_element_type=F32)))


def peer_scores(x, gain, scale, shift, mhi, mlo, prompt_tiles, tiles_per_sample):
    n, d = x.shape
    rows = mhi.shape[0]
    seg = lambda i: (_seg_of_tile(i, prompt_tiles, tiles_per_sample), 0, 0)
    return pl.pallas_call(
        _peer_scores_kernel,
        grid=(n // ROW_TILE,),
        in_specs=[pl.BlockSpec((ROW_TILE, d), lambda i: (i, 0)),
                  pl.BlockSpec((1, d), lambda i: (0, 0)),
                  pl.BlockSpec((1, 1, d), seg),
                  pl.BlockSpec((1, 1, d), seg),
                  pl.BlockSpec((rows, d), lambda i: (0, 0)),
                  pl.BlockSpec((rows, d), lambda i: (0, 0))],
        out_specs=[pl.BlockSpec((d, ROW_TILE), lambda i: (0, i)),
                   pl.BlockSpec((rows, ROW_TILE), lambda i: (0, i))],
        out_shape=[jax.ShapeDtypeStruct((d, n), BF16),
                   jax.ShapeDtypeStruct((rows, n), F32)],
        compiler_params=_cparams(("parallel",)),
    )(x, gain.reshape(1, d), scale, shift, mhi, mlo)


def _peer_select_kernel(s_ref, cnt_ref, e1_ref, rank_ref, e2_ref, v1_scr, v2_scr, st_scr):
    neg = -jnp.inf
    nr = PEER_RANKS
    for h in range(P_HEADS):
        for half, scr in ((0, v1_scr), (1, v2_scr)):
            s = s_ref[half, h]
            m = jnp.max(s, axis=0, keepdims=True)
            scr[0, pl.ds(h, 1), :] = m
            for rnk in range(1, nr):
                m = jnp.max(jnp.where(s < m, s, neg), axis=0, keepdims=True)
                scr[rnk, pl.ds(h, 1), :] = m
    cands = []
    for a in range(nr):
        for b in range(nr):
            if (a + 1) * (b + 1) <= nr:
                cands.append(v1_scr[a] + v2_scr[b])
    top = cands[0]
    m = top
    for rnk in range(1, nr):
        prev = m
        m = None
        for cnd in cands:
            x = jnp.where(cnd < prev, cnd, neg)
            m = x if m is None else jnp.maximum(m, x)
        if rnk == nr - 2:
            t16 = m
    t17 = m
    tau = 0.5 * (t16 + t17)
    zsum = jnp.zeros_like(top)
    for cnd in cands:
        zsum = zsum + jnp.where(cnd > tau, jnp.exp(cnd - top), 0.0)
    st_scr[0] = tau
    st_scr[1] = 1.0 / zsum
    for h in range(P_HEADS):
        s1 = s_ref[0, h]
        s2 = s_ref[1, h]
        thr = st_scr[0, pl.ds(h, 1), :] - s1
        cnt = jnp.zeros_like(s1)
        rank = jnp.zeros_like(s2)
        for b in range(nr):
            v2b = v2_scr[b, pl.ds(h, 1), :]
            cnt = cnt + jnp.where(v2b >= thr, 1.0, 0.0)
            rank = rank + jnp.where(v2b > s2, 1.0, 0.0)
        cnt_ref[h] = cnt
        rank_ref[h] = pltpu.bitcast(rank.astype(BF16), jnp.uint32)
        e1_ref[h] = jnp.exp(s1 - v1_scr[0, pl.ds(h, 1), :])
        e2 = jnp.exp(s2 - v2_scr[0, pl.ds(h, 1), :]) * st_scr[1, pl.ds(h, 1), :]
        e2_ref[h] = pltpu.bitcast(e2.astype(BF16), jnp.uint32)


def peer_select(s4):
    n = s4.shape[-1]
    tl = PEER_SEL_TILE
    ospec = pl.BlockSpec((P_HEADS, P_NKEYS, tl), lambda i: (0, 0, i))
    f32s = jax.ShapeDtypeStruct((P_HEADS, P_NKEYS, n), F32)
    pspec = pl.BlockSpec((P_HEADS, P_NKEYS // 2, tl), lambda i: (0, 0, i))
    packed = jax.ShapeDtypeStruct((P_HEADS, P_NKEYS // 2, n), jnp.uint32)
    return pl.pallas_call(
        _peer_select_kernel,
        grid=(n // tl,),
        in_specs=[pl.BlockSpec((2, P_HEADS, P_NKEYS, tl), lambda i: (0, 0, 0, i))],
        out_specs=[ospec, ospec, pspec, pspec], out_shape=[f32s, f32s, packed, packed],
        scratch_shapes=[pltpu.VMEM((PEER_RANKS, P_HEADS, tl), F32),
                        pltpu.VMEM((PEER_RANKS, P_HEADS, tl), F32),
                        pltpu.VMEM((2, P_HEADS, tl), F32)],
        compiler_params=_cparams(("parallel",)),
    )(s4)


def _peer_expert_kernel(ht_ref, u_ref, vt_ref, cnt_ref, e1_ref, rank_ref, e2_ref, x_ref, gate_ref,
                        o_ref, acc_scr, w_scr, *, n_eb):
    j = pl.program_id(1)

    @pl.when(j == 0)
    def _():
        acc_scr[...] = jnp.zeros_like(acc_scr)

    n_i = PEER_EXP_TILE // P_NKEYS
    sub = 16
    n_jt = P_NKEYS // sub
    for il in range(n_i):
        for lt in range(PEER_TOK_TILE // LANES):
            ls = pl.ds(lt * LANES, LANES)
            w = [None] * n_jt
            for h in range(P_HEADS):
                cb = jnp.broadcast_to(cnt_ref[h, pl.ds(il, 1), ls], (sub, LANES)).astype(BF16)
                eb = jnp.broadcast_to(e1_ref[h, pl.ds(il, 1), ls], (sub, LANES)).astype(BF16)
                for jt in range(n_jt):
                    rs = pl.ds(jt * (sub // 2), sub // 2)
                    rank = pltpu.bitcast(rank_ref[h, rs, ls], BF16)
                    e2 = pltpu.bitcast(e2_ref[h, rs, ls], BF16)
                    sel = jnp.where(rank < cb, e2, jnp.zeros((), BF16)) * eb
                    w[jt] = sel if w[jt] is None else w[jt] + sel
            for jt in range(n_jt):
                w_scr[pl.ds(il * P_NKEYS + jt * sub, sub), ls] = w[jt]
    at = jnp.dot(u_ref[...], ht_ref[...], preferred_element_type=F32)
    p = w_scr[...] * jax.nn.gelu(at).astype(BF16)
    acc_scr[...] += jnp.dot(vt_ref[...], p, preferred_element_type=F32)

    @pl.when(j == n_eb - 1)
    def _():
        o_ref[...] = x_ref[...] + gate_ref[0] * acc_scr[...].T


def peer_experts(ht, u_bf16, vt_bf16, cnt, e1, rank, e2, x, gate, prompt_tiles, tiles_per_sample):
    n, d = x.shape
    tt, eb = PEER_TOK_TILE, PEER_EXP_TILE
    n_eb = P_EXPERTS // eb
    n_i = eb // P_NKEYS
    scale = tt // ROW_TILE
    seg = lambda i, j: (_seg_of_tile(i * scale, prompt_tiles, tiles_per_sample), 0, 0)
    full = pl.BlockSpec((P_HEADS, P_NKEYS // 2, tt), lambda i, j: (0, 0, i))
    part = pl.BlockSpec((P_HEADS, n_i, tt), lambda i, j: (0, j, i))
    return pl.pallas_call(
        functools.partial(_peer_expert_kernel, n_eb=n_eb),
        grid=(n // tt, n_eb),
        in_specs=[pl.BlockSpec((d, tt), lambda i, j: (0, i)),
                  pl.BlockSpec((eb, d), lambda i, j: (j, 0)),
                  pl.BlockSpec((d, eb), lambda i, j: (0, j)),
                  part, part, full, full,
                  pl.BlockSpec((tt, d), lambda i, j: (i, 0)),
                  pl.BlockSpec((1, 1, d), seg)],
        out_specs=pl.BlockSpec((tt, d), lambda i, j: (i, 0)),
        out_shape=jax.ShapeDtypeStruct((n, d), F32),
        scratch_shapes=[pltpu.VMEM((d, tt), F32), pltpu.VMEM((eb, tt), BF16)],
        compiler_params=_cparams(("parallel", "arbitrary")),
    )(ht, u_bf16, vt_bf16, cnt, e1, rank, e2, x, gate)


def _final_norm_kernel(x_ref, g_ref, o_ref):
    x = x_ref[...]
    o_ref[...] = x * lax.rsqrt(jnp.mean(x * x, axis=-1, keepdims=True) + EPS) * g_ref[...]


def final_norm(x, gain):
    n, d = x.shape
    return pl.pallas_call(
        _final_norm_kernel,
        grid=(n // ROW_TILE,),
        in_specs=[pl.BlockSpec((ROW_TILE, d), lambda i: (i, 0)), pl.BlockSpec((1, d), lambda i: (0, 0))],
        out_specs=pl.BlockSpec((ROW_TILE, d), lambda i: (i, 0)),
        out_shape=jax.ShapeDtypeStruct((n, d), F32),
        compiler_params=_cparams(("parallel",)),
    )(x, gain.reshape(1, d))


def trunk(x_prompt, x_sample, state_hgrn, state_rwkv, c, c_ctx, params):
    bp, tp, d = x_prompt.shape
    bs, ts, _ = x_sample.shape
    n_p, n_s = bp * tp, bs * ts
    prompt_tiles, tiles_per_sample = n_p // ROW_TILE, ts // ROW_TILE
    tiling = (prompt_tiles, tiles_per_sample)
    x = jnp.concatenate([x_prompt.reshape(n_p, d), x_sample.reshape(n_s, d)], axis=0)
    cvec = jnp.concatenate([c_ctx[None], c], axis=0)

    lb_p = jax.nn.softmax(params['hgrn_lb_logits'], axis=1)
    lb = jnp.cumsum(lb_p, axis=1)
    lb = lb - lb[:, :1]
    tables = {tp: dft_tables(tp), ts: dft_tables(ts)}
    zero_a = jnp.zeros((bp, 2, A_HEADS, A_DK, A_DV), F32)
    zero_c = jnp.zeros((bp, 2, C_HEADS, C_HEAD_DIM, C_HEAD_DIM), F32)

    new_a, new_c = [], []
    for l in range(DEPTH):
        lp = {k: v[l] for k, v in params.items() if k not in ('hgrn_lb_logits', 'final_norm')}
        mod = jnp.dot(jax.nn.silu(cvec), lp['w_ada'], precision=lax.Precision.HIGHEST) + lp['b_ada']
        sh1, sc1, g1, sh2, sc2, g2 = [m[:, None, :] for m in jnp.split(mod, 6, axis=-1)]

        w_in = lp['w_in'].astype(BF16)
        splits = np.cumsum([0, A_COLS, B_COLS, C_COLS, G_COLS])
        rp = rwkv_permuted_params(lp, w_in[:, splits[2]:splits[3]])
        w_parts = [w_in[:, splits[0]:splits[1]], w_in[:, splits[1]:splits[2]], rp['w_c'], w_in[:, splits[3]:]]
        pa, pb, pc, pg = [norm_mod_matmul(x, lp['norm1'], sc1, sh1, w, *tiling) for w in w_parts]

        lb_l = lb[:, l][:, None, :]
        oa_p, sa_p = hgrn2_scan(pa, lb_l, zero_a, 0, bp, tp)
        oa_s, _ = hgrn2_scan(pa, lb_l, state_hgrn[:, l], n_p, bs, ts)
        o_hgrn = jnp.concatenate([oa_p, oa_s], axis=2)

        yb = jnp.concatenate([hyena_branch(pb[:n_p].reshape(bp, tp, B_COLS), lp, tables[tp]).reshape(n_p, -1),
                              hyena_branch(pb[n_p:].reshape(bs, ts, B_COLS), lp, tables[ts]).reshape(n_s, -1)])

        ptf, ptb, rw_gate, bonus = rwkv_pre(pc, rp, prompt_tiles, tp // ROW_TILE, tiles_per_sample, tp, ts)
        yr_p, sc_p = rwkv7_scan_segment(ptf, ptb, zero_c, 0, bp, tp)
        yr_s, _ = rwkv7_scan_segment(ptf, ptb, state_rwkv[:, l], n_p, bs, ts)

        wb = jnp.stack([lp['w_branch'][0], lp['w_branch'][1], rp['w_branch_c']]).astype(BF16)
        x = merge_out(x, o_hgrn, pa, lp['hgrn_norm'], yb, yr_p, yr_s, bonus, rw_gate, rp['rw_ln_g'], rp['rw_ln_b'],
                      pg, g1, wb, lp['w_out'].astype(BF16), *tiling)

        mhi, mlo = peer_score_matrix(lp['p_wq'], lp['p_key1'], lp['p_key2'])
        ht, st = peer_scores(x, lp['norm2'], sc2, sh2, mhi, mlo, *tiling)
        s4 = st.reshape(2, P_HEADS, P_NKEYS, n_p + n_s)
        cnt, e1, rank, e2 = peer_select(s4)
        x = peer_experts(ht, lp['p_u'].astype(BF16), lp['p_v'].T.astype(BF16), cnt, e1, rank, e2, x, g2, *tiling)

        new_a.append(sa_p)
        new_c.append(sc_p)

    y = final_norm(x, params['final_norm'])
    return (y[:n_p].reshape(bp, tp, d), y[n_p:].reshape(bs, ts, d),
            jnp.stack(new_a, axis=1), jnp.stack(new_c, axis=1))


def kernel(x_prompt, x_sample, state_hgrn, state_rwkv, c, c_ctx, w_ada, b_ada, norm1, norm2, w_in,
           hgrn_lb_logits, hgrn_norm, hy_conv_w, hy_conv_b, hy_w1, hy_b1, hy_freq1, hy_w2, hy_b2,
           hy_freq2, hy_w3, hy_delta, hy_bias, rw_mu, rw_w0, rw_w2, rw_a0, rw_a2, rw_g2, rw_kk, rw_ka,
           rw_rk, rw_ln_g, rw_ln_b, w_branch, w_out, p_wq, p_key1, p_key2, p_u, p_v, final_norm):
    params = dict(w_ada=w_ada, b_ada=b_ada, norm1=norm1, norm2=norm2, w_in=w_in,
                  hgrn_lb_logits=hgrn_lb_logits, hgrn_norm=hgrn_norm, hy_conv_w=hy_conv_w,
                  hy_conv_b=hy_conv_b, hy_w1=hy_w1, hy_b1=hy_b1, hy_freq1=hy_freq1, hy_w2=hy_w2,
                  hy_b2=hy_b2, hy_freq2=hy_freq2, hy_w3=hy_w3, hy_delta=hy_delta, hy_bias=hy_bias,
                  rw_mu=rw_mu, rw_w0=rw_w0, rw_w2=rw_w2, rw_a0=rw_a0, rw_a2=rw_a2, rw_g2=rw_g2,
                  rw_kk=rw_kk, rw_ka=rw_ka, rw_rk=rw_rk, rw_ln_g=rw_ln_g, rw_ln_b=rw_ln_b,
                  w_branch=w_branch, w_out=w_out, p_wq=p_wq, p_key1=p_key1, p_key2=p_key2,
                  p_u=p_u, p_v=p_v, final_norm=final_norm)
    return trunk(x_prompt, x_sample, state_hgrn, state_rwkv, c, c_ctx, params)
```

```python
import functools
import math
import jax
import jax.numpy as jnp
from jax import lax
import numpy as np
from jax.experimental import pallas as pl
from jax.experimental.pallas import tpu as pltpu

D_MODEL = 1024
DEPTH = 4
GRID_W = 64
BRANCH_WIDTH = 512
N_BRANCH = 3

A_DK = 128
A_DV = 128
A_HEADS = BRANCH_WIDTH // A_DV

HY_BANDS = 16
HY_EMB = 1 + 2 * HY_BANDS
HY_FFN = 64

C_HEAD_DIM = 64
C_HEADS = BRANCH_WIDTH // C_HEAD_DIM
C_WIDTH = BRANCH_WIDTH
C_DECAY_LORA = 64
C_AAA_LORA = 64
C_GATE_LORA = 128

A_COLS = 3 * A_HEADS * A_DK + A_HEADS * A_DV + BRANCH_WIDTH
B_COLS = 3 * BRANCH_WIDTH
C_COLS = 3 * C_WIDTH + 2 * C_DECAY_LORA + 2 * C_AAA_LORA + C_GATE_LORA
G_COLS = N_BRANCH * D_MODEL

P_HEADS = 8
P_NKEYS = 128
P_EXPERTS = P_NKEYS * P_NKEYS
P_TOPK = 16
P_KEY_DIM = 128

EPS = 1e-6
RWKV_GN_EPS = 64e-5

LANES = 128
ROW_TILE = 256
VMEM_LIMIT = 56 * 1024 * 1024

F32 = jnp.float32
BF16 = jnp.bfloat16


def _cparams(sem):
    return pltpu.CompilerParams(dimension_semantics=sem, vmem_limit_bytes=VMEM_LIMIT)


def _bdot(a, b):
    return jnp.dot(a.astype(BF16), b.astype(BF16), preferred_element_type=F32)


def _split_hi_lo(x):
    hi = x.astype(BF16)
    lo = (x - hi.astype(F32)).astype(BF16)
    return hi, lo


def _seg_of_tile(i, prompt_tiles, tiles_per_sample):
    return jnp.where(i < prompt_tiles, 0, 1 + (i - prompt_tiles) // tiles_per_sample)


def _norm_mod(x, gain, scale, shift):
    return x * lax.rsqrt(jnp.mean(x * x, axis=-1, keepdims=True) + EPS) * gain * (1.0 + scale) + shift


def _inproj_kernel(x_ref, gain_ref, scale_ref, shift_ref, w_ref, o_ref):
    h = _norm_mod(x_ref[...], gain_ref[...], scale_ref[0], shift_ref[0])
    o_ref[...] = jnp.dot(h.astype(BF16), w_ref[...], preferred_element_type=F32)


def norm_mod_matmul(x, gain, scale, shift, w_bf16, prompt_tiles, tiles_per_sample):
    n, d = x.shape
    cols = w_bf16.shape[1]
    seg = lambda i: (_seg_of_tile(i, prompt_tiles, tiles_per_sample), 0, 0)
    return pl.pallas_call(
        _inproj_kernel,
        grid=(n // ROW_TILE,),
        in_specs=[pl.BlockSpec((ROW_TILE, d), lambda i: (i, 0)),
                  pl.BlockSpec((1, d), lambda i: (0, 0)),
                  pl.BlockSpec((1, 1, d), seg),
                  pl.BlockSpec((1, 1, d), seg),
                  pl.BlockSpec((d, cols), lambda i: (0, 0))],
        out_specs=pl.BlockSpec((ROW_TILE, cols), lambda i: (i, 0)),
        out_shape=jax.ShapeDtypeStruct((n, cols), F32),
        compiler_params=_cparams(("parallel",)),
    )(x, gain.reshape(1, d), scale, shift, w_bf16)


HG_CHUNK = 128
HG_LEVELS = 7


def _hgrn_tables():
    c = HG_CHUNK
    out = np.zeros((2, HG_LEVELS + 2, c, c), np.float32)
    t = np.arange(c)
    for d in range(2):
        pos = t if d == 0 else c - 1 - t
        pt, pu = pos[:, None], pos[None, :]
        for lv in range(HG_LEVELS):
            m = c >> (lv + 1)
            same = (pt // m) == (pu // m)
            query = ((pt // m) % 2) == 1
            out[d, lv] = same & np.where(query, pu <= pt, pu > pt)
        out[d, HG_LEVELS] = pu <= pt
        out[d, HG_LEVELS + 1] = pu > pt
    return out.reshape(2, (HG_LEVELS + 2) * c, c)


def _hgrn_kernel(q_ref, fz_ref, v_ref, lb_ref, tab_ref, s0_ref, o_ref, sfin_ref, st_scr, *, n_chunks):
    c = HG_CHUNK
    d = pl.program_id(1)
    ci = pl.program_id(2)

    @pl.when(ci == 0)
    def _():
        for h in range(A_HEADS):
            st_scr[h] = s0_ref[0, 0, h].T

    row = lax.broadcasted_iota(jnp.int32, (c, LANES), 0)
    pos_r = row + d * (c - 1 - 2 * row)
    ti = lax.broadcasted_iota(jnp.int32, (c, c), 0)
    si = lax.broadcasted_iota(jnp.int32, (c, c), 1)
    pos_t = ti + d * (c - 1 - 2 * ti)
    pos_s = si + d * (c - 1 - 2 * si)
    tab = tab_ref[0]

    for h in range(A_HEADS):
        hs = pl.ds(h * LANES, LANES)
        qz = q_ref[:, hs]
        q = qz * jax.nn.sigmoid(qz)
        lb = lb_ref[0, :, hs]
        f = lb + (1.0 - lb) * jax.nn.sigmoid(fz_ref[:, hs])
        kd = 1.0 - f
        logf = jnp.log(f)
        v = v_ref[:, hs]

        lhi, llo = _split_hi_lo(logf)
        args = (jnp.dot(tab, lhi, preferred_element_type=F32)
                + jnp.dot(tab, llo, preferred_element_type=F32))
        e_all = jnp.exp(args)

        attn = jnp.where(ti == si, jnp.sum(q * kd, axis=-1, keepdims=True), 0.0)
        for lv in range(HG_LEVELS):
            sh = HG_LEVELS - 1 - lv
            e = e_all[lv * c:(lv + 1) * c]
            is_q = ((pos_r >> sh) & 1) == 1
            qs = jnp.where(is_q, q * e, 0.0).astype(BF16)
            ks = jnp.where(is_q, 0.0, kd * e).astype(BF16)
            sc = lax.dot_general(qs, ks, (((1,), (1,)), ((), ())), preferred_element_type=F32)
            attn = attn + jnp.where((pos_t >> (sh + 1)) == (pos_s >> (sh + 1)), sc, 0.0)

        e_cum = e_all[HG_LEVELS * c:(HG_LEVELS + 1) * c]
        e_rev = e_all[(HG_LEVELS + 1) * c:]
        st = st_scr[h]
        o = _bdot(attn, v) + lax.dot_general((q * e_cum).astype(BF16), st.astype(BF16),
                                             (((1,), (1,)), ((), ())), preferred_element_type=F32)
        o_ref[0, 0, :, hs] = o
        dec = jnp.exp(jnp.sum(logf, axis=0, keepdims=True))
        st_new = st * dec + _bdot(v.T, kd * e_rev)
        st_scr[h] = st_new

        @pl.when(ci == n_chunks - 1)
        def _():
            sfin_ref[0, 0, h] = st_new.T


def hgrn2_scan(pa, lb, s0, row_off, batch, seqlen):
    c = HG_CHUNK
    n_chunks = seqlen // c
    base = row_off // c
    hw = A_HEADS * A_DK
    tab = jnp.asarray(_hgrn_tables(), BF16)

    def rows(b, ci, d):
        return base + b * n_chunks + ci + d * (n_chunks - 1 - 2 * ci)

    def orow(b, ci, d):
        return b * n_chunks + ci + d * (n_chunks - 1 - 2 * ci)

    return pl.pallas_call(
        functools.partial(_hgrn_kernel, n_chunks=n_chunks),
        grid=(batch, 2, n_chunks),
        in_specs=[pl.BlockSpec((c, hw), lambda b, d, ci: (rows(b, ci, d), 0)),
                  pl.BlockSpec((c, hw), lambda b, d, ci: (rows(b, ci, d), 1 + d)),
                  pl.BlockSpec((c, hw), lambda b, d, ci: (rows(b, ci, d), 3)),
                  pl.BlockSpec((1, 1, hw), lambda b, d, ci: (d, 0, 0)),
                  pl.BlockSpec((1, (HG_LEVELS + 2) * c, c), lambda b, d, ci: (d, 0, 0)),
                  pl.BlockSpec((1, 1, A_HEADS, A_DK, A_DV), lambda b, d, ci: (b, d, 0, 0, 0))],
        out_specs=[pl.BlockSpec((1, 1, c, hw), lambda b, d, ci: (d, 0, orow(b, ci, d), 0)),
                   pl.BlockSpec((1, 1, A_HEADS, A_DK, A_DV), lambda b, d, ci: (b, d, 0, 0, 0))],
        out_shape=[jax.ShapeDtypeStruct((2, 1, batch * seqlen, A_HEADS * A_DV), F32),
                   jax.ShapeDtypeStruct((batch, 2, A_HEADS, A_DK, A_DV), F32)],
        scratch_shapes=[pltpu.VMEM((A_HEADS, A_DV, A_DK), F32)],
        compiler_params=_cparams(("parallel", "parallel", "arbitrary")),
    )(pa, pa, pa, lb, tab, s0)


def _dft_table_kernel(fc_ref, fs_ref, gc_ref, gs_ref, *, length, tile):
    n2 = 2 * length
    a = lax.broadcasted_iota(jnp.int32, (tile, length), 0) + pl.program_id(0) * tile
    b = lax.broadcasted_iota(jnp.int32, (tile, length), 1)
    ang = ((a * b) & (n2 - 1)).astype(F32) * (2.0 * math.pi / n2)
    co, si = jnp.cos(ang), jnp.sin(ang)
    alt_b = (1 - 2 * (b & 1)).astype(F32)
    alt_a = (1 - 2 * (a & 1)).astype(F32)
    fc_ref[...] = co.astype(BF16)
    fs_ref[...] = jnp.where(a == 0, alt_b, -si).astype(BF16)
    wk = jnp.where(b == 0, 1.0 / n2, 2.0 / n2)
    gc_ref[...] = (wk * co).astype(BF16)
    gs_ref[...] = jnp.where(b == 0, alt_a * (1.0 / n2), -wk * si).astype(BF16)


def dft_tables(length):
    tile = min(length, 256)
    spec = pl.BlockSpec((tile, length), lambda i: (i, 0))
    shp = jax.ShapeDtypeStruct((length, length), BF16)
    return pl.pallas_call(
        functools.partial(_dft_table_kernel, length=length, tile=tile),
        grid=(length // tile,),
        out_specs=[spec] * 4, out_shape=[shp] * 4,
        compiler_params=_cparams(("parallel",)),
    )()


def _dft_fwd_kernel(fc_ref, fs_ref, z_ref, zr_ref, zi_ref):
    z = z_ref[0].astype(BF16)
    zr_ref[0] = jnp.dot(fc_ref[...], z, preferred_element_type=F32)
    zi_ref[0] = jnp.dot(fs_ref[...], z, preferred_element_type=F32)


def dft_forward(fc, fs, z):
    batch, length, ch = z.shape
    tile = min(length, 512)
    return pl.pallas_call(
        _dft_fwd_kernel,
        grid=(length // tile, batch),
        in_specs=[pl.BlockSpec((tile, length), lambda i, b: (i, 0)),
                  pl.BlockSpec((tile, length), lambda i, b: (i, 0)),
                  pl.BlockSpec((1, length, ch), lambda i, b: (b, 0, 0))],
        out_specs=[pl.BlockSpec((1, tile, ch), lambda i, b: (b, i, 0))] * 2,
        out_shape=[jax.ShapeDtypeStruct((batch, length, ch), F32)] * 2,
        compiler_params=_cparams(("parallel", "arbitrary")),
    )(fc, fs, z)


def _spec_mul_kernel(zr_ref, zi_ref, kr_ref, ki_ref, yr_ref, yi_ref, *, tile):
    row = lax.broadcasted_iota(jnp.int32, (tile, 1), 0) + pl.program_id(1) * tile
    sgn = (1 - 2 * (row & 1)).astype(F32)
    kr = kr_ref[0] + sgn * kr_ref[1]
    ki = ki_ref[0] + sgn * ki_ref[1]
    zr, zi = zr_ref[0], zi_ref[0]
    packed = row == 0
    yr_ref[0] = jnp.where(packed, zr * kr, zr * kr - zi * ki).astype(BF16)
    yi_ref[0] = jnp.where(packed, zi * ki, zr * ki + zi * kr).astype(BF16)


def spectrum_multiply(zr, zi, kr, ki):
    batch, length, ch = zr.shape
    tile = min(length, 512)
    zspec = pl.BlockSpec((1, tile, ch), lambda b, i: (b, i, 0))
    kspec = pl.BlockSpec((2, tile, ch), lambda b, i: (0, i, 0))
    return pl.pallas_call(
        functools.partial(_spec_mul_kernel, tile=tile),
        grid=(batch, length // tile),
        in_specs=[zspec, zspec, kspec, kspec],
        out_specs=[zspec, zspec],
        out_shape=[jax.ShapeDtypeStruct((batch, length, ch), BF16)] * 2,
        compiler_params=_cparams(("parallel", "parallel")),
    )(zr, zi, kr, ki)


def _dft_inv_kernel(gc_ref, gs_ref, yr_ref, yi_ref, x0_ref, z_ref, bias_ref, o_ref):
    y = (jnp.dot(gc_ref[...], yr_ref[0], preferred_element_type=F32)
         + jnp.dot(gs_ref[...], yi_ref[0], preferred_element_type=F32))
    o_ref[0] = x0_ref[0] * (y + z_ref[0] * bias_ref[...])


def dft_inverse_gate(gc, gs, yr, yi, x0, z, bias):
    batch, length, ch = yr.shape
    tile = min(length, 512)
    tspec = pl.BlockSpec((1, tile, ch), lambda i, b: (b, i, 0))
    return pl.pallas_call(
        _dft_inv_kernel,
        grid=(length // tile, batch),
        in_specs=[pl.BlockSpec((tile, length), lambda i, b: (i, 0)),
                  pl.BlockSpec((tile, length), lambda i, b: (i, 0)),
                  pl.BlockSpec((1, length, ch), lambda i, b: (b, 0, 0)),
                  pl.BlockSpec((1, length, ch), lambda i, b: (b, 0, 0)),
                  tspec, tspec,
                  pl.BlockSpec((1, ch), lambda i, b: (0, 0))],
        out_specs=tspec,
        out_shape=jax.ShapeDtypeStruct((batch, length, ch), F32),
        compiler_params=_cparams(("parallel", "arbitrary")),
    )(gc, gs, yr, yi, x0, z, bias.reshape(1, ch))


def shift_prev(x):
    return jnp.pad(x, ((0, 0), (1, 0), (0, 0)))[:, :-1]


def shift_next(x):
    return jnp.pad(x, ((0, 0), (0, 1), (0, 0)))[:, 1:]


def hyena_filter(length, lp):
    hp = lax.Precision.HIGHEST
    t = jnp.linspace(0.0, 1.0, length, dtype=F32)[:, None]
    bands = jnp.linspace(1e-4, HY_BANDS - 1, HY_BANDS, dtype=F32)[None, :]
    ang = (2 * math.pi / length) * jnp.arange(length, dtype=F32)[:, None] * bands
    z = jnp.concatenate([t, jnp.cos(ang), -jnp.sin(ang)], axis=-1)
    hf = jnp.sin(lp['hy_freq1'] * (jnp.dot(z, lp['hy_w1'], precision=hp) + lp['hy_b1']))
    hf = jnp.sin(lp['hy_freq2'] * (jnp.dot(hf, lp['hy_w2'], precision=hp) + lp['hy_b2']))
    hf = jnp.dot(hf, lp['hy_w3'], precision=hp).reshape(length, 2, BRANCH_WIDTH)
    hf = hf * jnp.exp(-t * jnp.abs(lp['hy_delta']))[:, None, :]
    lo = hf[:, 0]
    hi = jnp.concatenate([jnp.zeros((1, BRANCH_WIDTH), F32), hf[:0:-1, 1]], axis=0)
    norm = jnp.sum(jnp.abs(lo), axis=0, keepdims=True) + jnp.sum(jnp.abs(hi), axis=0, keepdims=True)
    return jnp.stack([lo, hi]) / norm


def hyena_branch(pb, lp, tables):
    batch, length, _ = pb.shape
    fc, fs, gc, gs = tables
    cw = lp['hy_conv_w']
    u = shift_prev(pb) * cw[0] + pb * cw[1] + shift_next(pb) * cw[2] + lp['hy_conv_b']
    x0, x1, v = jnp.split(u, 3, axis=-1)
    z = v * x1
    kr, ki = dft_forward(fc, fs, hyena_filter(length, lp))
    zr, zi = dft_forward(fc, fs, z)
    yr, yi = spectrum_multiply(zr, zi, kr, ki)
    return dft_inverse_gate(gc, gs, yr, yi, x0, z, lp['hy_bias'])


RW_GROUP_BATCH = 4
RW_VROWS = C_HEAD_DIM // 2
RW_TCHUNK = 32
RW_STACK = 6


def _head_sum(y, bd):
    hi, lo = _split_hi_lo(y)
    return jnp.dot(hi, bd, preferred_element_type=F32) + jnp.dot(lo, bd, preferred_element_type=F32)


def _split3(x):
    hi = x.astype(BF16)
    r1 = x - hi.astype(F32)
    mid = r1.astype(BF16)
    return hi, mid, (r1 - mid.astype(F32)).astype(BF16)


def _reverse_lanes(x, flip):
    hi, mid, lo = _split3(x)
    dot = lambda p: jnp.dot(p, flip, preferred_element_type=F32)
    return (dot(hi) + dot(mid)) + dot(lo)


def _exchange(n):
    return jnp.asarray(np.eye(n)[::-1], BF16)


def _head_ones():
    h = np.arange(C_WIDTH) % C_HEADS
    return jnp.asarray(h[:, None] == h[None, :], BF16)


RW_PERM = np.arange(C_WIDTH).reshape(C_HEADS, C_HEAD_DIM).T.reshape(-1)


def rwkv_permuted_params(lp, w_c):
    p = RW_PERM
    cols = np.concatenate([p, C_WIDTH + p, 2 * C_WIDTH + p, np.arange(3 * C_WIDTH, C_COLS)])
    return dict(w_c=w_c[:, cols], rw_mu=lp['rw_mu'][cols], rw_kk=lp['rw_kk'][p], rw_ka=lp['rw_ka'][p],
                rw_rk=lp['rw_rk'].reshape(-1)[p], rw_w0=lp['rw_w0'][:, p], rw_w2=lp['rw_w2'][:, :, p],
                rw_a0=lp['rw_a0'][:, p], rw_a2=lp['rw_a2'][:, :, p], rw_g2=lp['rw_g2'][:, p],
                rw_ln_g=lp['rw_ln_g'][p], rw_ln_b=lp['rw_ln_b'][p], w_branch_c=lp['w_branch'][2][p])


def _rwkv_pre_kernel(prv_ref, cur_ref, nxt_ref, mu_ref, kks_ref, ka_ref, rk_ref, w0_ref, w2_ref, a0_ref,
                     a2_ref, g2_ref, bd_ref, flip_ref, *out_refs, prompt_tiles, tiles_per_prompt, tiles_per_sample,
                     prompt_len, sample_len):
    i = pl.program_id(0)
    tm = ROW_TILE
    latent = i >= prompt_tiles
    tile_in_seq = jnp.where(latent, (i - prompt_tiles) % tiles_per_sample, i % tiles_per_prompt)
    seq_len = jnp.where(latent, sample_len, prompt_len)
    row = lax.broadcasted_iota(jnp.int32, (tm, 1), 0)
    pos = tile_in_seq * tm + row
    col = pos % GRID_W
    one = lambda c: jnp.where(c, 1.0, 0.0)
    m_left = jnp.where(latent, one(col != 0), one(pos != 0))
    m_right = jnp.where(latent, one(col != GRID_W - 1), one(pos != seq_len - 1))
    m_up = one(pos >= GRID_W)
    m_down = one(pos < seq_len - GRID_W)
    c_lr = jnp.where(latent, 0.25, 0.5)
    c_ud = jnp.where(latent, 0.25, 0.0)

    cur = cur_ref[...]
    prv = prv_ref[...]
    nxt = nxt_ref[...]
    left = jnp.where(row == 0, prv[tm - 1:tm], pltpu.roll(cur, 1, axis=0))
    right = jnp.where(row == tm - 1, nxt[0:1], pltpu.roll(cur, tm - 1, axis=0))
    up = jnp.concatenate([prv[tm - GRID_W:], cur[:tm - GRID_W]], axis=0)
    down = jnp.concatenate([cur[GRID_W:], nxt[:GRID_W]], axis=0)
    nm = c_lr * (left * m_left + right * m_right) + c_ud * (up * m_up + down * m_down)
    x = cur + (nm - cur) * mu_ref[...]

    cw = C_WIDTH
    r, k, v = x[:, :cw], x[:, cw:2 * cw], x[:, 2 * cw:3 * cw]
    o = 3 * cw
    wd = x[:, o:o + 2 * C_DECAY_LORA]
    ad = x[:, o + 2 * C_DECAY_LORA:o + 2 * C_DECAY_LORA + 2 * C_AAA_LORA]
    gd = x[:, o + 2 * C_DECAY_LORA + 2 * C_AAA_LORA:]
    bd = bd_ref[...]

    kk = k * kks_ref[...]
    kkn = kk * lax.rsqrt(_head_sum(kk * kk, bd) + 1e-12)
    ptf_o, ptb_o, g_o, bonus_o = out_refs
    flip = flip_ref[...]
    for idx, arr in enumerate((r, v, -kkn)):
        at = arr.T
        ptf_o[idx] = at
        ptb_o[idx] = _reverse_lanes(at, flip)
    g_o[...] = _bdot(jax.nn.sigmoid(gd), g2_ref[...])
    bonus_o[...] = _head_sum(r * k * rk_ref[...], bd) * v
    for d, pt_o in enumerate((ptf_o, ptb_o)):
        z = -(w0_ref[d] + _bdot(jnp.tanh(wd[:, d * C_DECAY_LORA:(d + 1) * C_DECAY_LORA]), w2_ref[d]))
        softplus = jnp.maximum(z, 0.0) + jnp.log(1.0 + jnp.exp(-jnp.abs(z)))
        ag = jax.nn.sigmoid(a0_ref[d] + _bdot(ad[:, d * C_AAA_LORA:(d + 1) * C_AAA_LORA], a2_ref[d]))
        outs = (jnp.exp(-jnp.exp(-softplus - 0.5)), k * (1.0 + (ag - 1.0) * ka_ref[...]), kkn * ag)
        for idx, arr in enumerate(outs):
            at = arr.T
            pt_o[3 + idx] = at if d == 0 else _reverse_lanes(at, flip)


def rwkv_pre(pc, rp, prompt_tiles, tiles_per_prompt, tiles_per_sample, prompt_len, sample_len):
    n, cc = pc.shape
    cw = C_WIDTH
    nt = n // ROW_TILE
    vec = lambda a: a.reshape(1, -1)
    const2 = lambda shp: pl.BlockSpec(shp, lambda i: (0, 0))
    const3 = lambda shp: pl.BlockSpec(shp, lambda i: (0, 0, 0))
    ospec = pl.BlockSpec((ROW_TILE, cw), lambda i: (i, 0))

    def mirrored(i):
        per = jnp.where(i < prompt_tiles, tiles_per_prompt, tiles_per_sample)
        rel = jnp.where(i < prompt_tiles, i, i - prompt_tiles)
        return i - 2 * (rel % per) + per - 1

    kern = functools.partial(_rwkv_pre_kernel, prompt_tiles=prompt_tiles, tiles_per_prompt=tiles_per_prompt,
                             tiles_per_sample=tiles_per_sample, prompt_len=prompt_len, sample_len=sample_len)
    return pl.pallas_call(
        kern,
        grid=(nt,),
        in_specs=[pl.BlockSpec((ROW_TILE, cc), lambda i: (jnp.maximum(i - 1, 0), 0)),
                  pl.BlockSpec((ROW_TILE, cc), lambda i: (i, 0)),
                  pl.BlockSpec((ROW_TILE, cc), lambda i: (jnp.minimum(i + 1, nt - 1), 0)),
                  const2((1, cc)), const2((1, cw)), const2((1, cw)), const2((1, cw)),
                  const3((2, 1, cw)), const3((2, C_DECAY_LORA, cw)),
                  const3((2, 1, cw)), const3((2, C_AAA_LORA, cw)),
                  const2((C_GATE_LORA, cw)), const2((cw, cw)), const2((ROW_TILE, ROW_TILE))],
        out_specs=[pl.BlockSpec((RW_STACK, cw, ROW_TILE), lambda i: (0, 0, i)),
                   pl.BlockSpec((RW_STACK, cw, ROW_TILE), lambda i: (0, 0, mirrored(i))), ospec, ospec],
        out_shape=[jax.ShapeDtypeStruct((RW_STACK, cw, n), F32), jax.ShapeDtypeStruct((RW_STACK, cw, n), F32),
                   jax.ShapeDtypeStruct((n, cw), F32), jax.ShapeDtypeStruct((n, cw), F32)],
        compiler_params=_cparams(("parallel",)),
    )(pc, pc, pc, vec(rp['rw_mu']), vec(rp['rw_kk']), vec(rp['rw_ka']), vec(rp['rw_rk']),
      rp['rw_w0'][:, None, :], rp['rw_w2'].astype(BF16), rp['rw_a0'][:, None, :], rp['rw_a2'].astype(BF16),
      rp['rw_g2'].astype(BF16), _head_ones(), _exchange(ROW_TILE))


RW_RELAYOUT_T = 128
RW_KQ = 4
RW_KROWS = (0, 3, 4, 2, 5)


def _rwkv_lanes_kernel(f0, f1, f2, f3, b0, b1, b2, b3, kl_ref, vl_ref):
    q = pl.program_id(2)
    stacks = ((f0, f1, f2, f3), (b0, b1, b2, b3))
    nk = C_HEAD_DIM // RW_KQ
    for kl in range(nk):
        r0 = pl.multiple_of((q * nk + kl) * C_HEADS, C_HEADS)
        for ai, src in enumerate(RW_KROWS):
            base = [st[b][src, pl.ds(r0, C_HEADS), :] for st in stacks for b in range(RW_GROUP_BATCH)]
            kl_ref[0, ai, kl] = jnp.concatenate(base + base, axis=0).T
    nv = RW_VROWS // RW_KQ
    for vi in range(nv):
        pieces = []
        for vh in range(2):
            r0 = pl.multiple_of((vh * RW_VROWS + q * nv + vi) * C_HEADS, C_HEADS)
            pieces += [st[b][1, pl.ds(r0, C_HEADS), :] for st in stacks for b in range(RW_GROUP_BATCH)]
        vl_ref[0, vi] = jnp.concatenate(pieces, axis=0).T


def rwkv_to_lanes(ptf, ptb, off, batch, seqlen):
    g = batch // RW_GROUP_BATCH
    tt = RW_RELAYOUT_T
    nj = seqlen // tt
    col0 = off // tt
    nk, nv = C_HEAD_DIM // RW_KQ, RW_VROWS // RW_KQ

    def src(b):
        return pl.BlockSpec((RW_STACK, C_WIDTH, tt),
                            lambda gi, j, q: (0, 0, col0 + (gi * RW_GROUP_BATCH + b) * nj + j))

    specs = [src(b) for b in range(RW_GROUP_BATCH)]
    return pl.pallas_call(
        _rwkv_lanes_kernel,
        grid=(g, nj, RW_KQ),
        in_specs=specs + specs,
        out_specs=[pl.BlockSpec((1, 5, nk, tt, LANES), lambda gi, j, q: (gi, 0, q, j, 0)),
                   pl.BlockSpec((1, nv, tt, LANES), lambda gi, j, q: (gi, q, j, 0))],
        out_shape=[jax.ShapeDtypeStruct((g, 5, C_HEAD_DIM, seqlen, LANES), F32),
                   jax.ShapeDtypeStruct((g, RW_VROWS, seqlen, LANES), F32)],
        compiler_params=_cparams(("parallel", "parallel", "arbitrary")),
    )(ptf, ptf, ptf, ptf, ptb, ptb, ptb, ptb)


def _rwkv_scan_kernel(k_ref, v_ref, s0_ref, y_ref, sfin_ref, s_scr, row_scr, v_scr, y_scr, *, n_tc):
    tc = pl.program_id(1)

    @pl.when(tc == 0)
    def _():
        s_scr[...] = s0_ref[0]

    nk = C_HEAD_DIM
    R, W, KD, A, B = range(5)
    for t0 in range(0, RW_TCHUNK, 8):
        for a in range(5):
            for k0 in range(0, nk, 8):
                row_scr[pl.ds(t0, 8), a, pl.ds(k0, 8), :] = jnp.swapaxes(
                    k_ref[0, a, pl.ds(k0, 8), pl.ds(t0, 8), :], 0, 1)
        for r0 in range(0, RW_VROWS, 8):
            v_scr[pl.ds(t0, 8), pl.ds(r0, 8), :] = jnp.swapaxes(v_ref[0, pl.ds(r0, 8), pl.ds(t0, 8), :], 0, 1)

    def step(t, carry):
        parts = [jnp.zeros((RW_VROWS, LANES), F32) for _ in range(4)]
        for kk in range(nk):
            parts[kk % 4] = parts[kk % 4] + s_scr[kk] * row_scr[t, A, pl.ds(kk, 1), :]
        sa = (parts[0] + parts[1]) + (parts[2] + parts[3])
        v_t = v_scr[t]
        ys = [jnp.zeros((RW_VROWS, LANES), F32) for _ in range(4)]
        for kk in range(nk):
            s_new = (s_scr[kk] * row_scr[t, W, pl.ds(kk, 1), :] + sa * row_scr[t, B, pl.ds(kk, 1), :]
                     + v_t * row_scr[t, KD, pl.ds(kk, 1), :])
            s_scr[kk] = s_new
            ys[kk % 4] = ys[kk % 4] + s_new * row_scr[t, R, pl.ds(kk, 1), :]
        y_scr[t] = (ys[0] + ys[1]) + (ys[2] + ys[3])
        return carry

    lax.fori_loop(0, RW_TCHUNK, step, 0)
    for t0 in range(0, RW_TCHUNK, 8):
        for r0 in range(0, RW_VROWS, 8):
            y_ref[0, pl.ds(r0, 8), pl.ds(t0, 8), :] = jnp.swapaxes(y_scr[pl.ds(t0, 8), pl.ds(r0, 8), :], 0, 1)

    @pl.when(tc == n_tc - 1)
    def _():
        sfin_ref[0] = s_scr[...]


def rwkv7_scan_lanes(kl, vl, s0):
    g, t = vl.shape[0], vl.shape[2]
    n_tc = t // RW_TCHUNK
    kspec = pl.BlockSpec((1, 5, C_HEAD_DIM, RW_TCHUNK, LANES), lambda gi, ti: (gi, 0, 0, ti, 0))
    vspec = pl.BlockSpec((1, RW_VROWS, RW_TCHUNK, LANES), lambda gi, ti: (gi, 0, ti, 0))
    sspec = pl.BlockSpec((1, C_HEAD_DIM, RW_VROWS, LANES), lambda gi, ti: (gi, 0, 0, 0))
    return pl.pallas_call(
        functools.partial(_rwkv_scan_kernel, n_tc=n_tc),
        grid=(g, n_tc),
        in_specs=[kspec, vspec, sspec],
        out_specs=[vspec, sspec],
        out_shape=[jax.ShapeDtypeStruct((g, RW_VROWS, t, LANES), F32),
                   jax.ShapeDtypeStruct((g, C_HEAD_DIM, RW_VROWS, LANES), F32)],
        scratch_shapes=[pltpu.VMEM((C_HEAD_DIM, RW_VROWS, LANES), F32),
                        pltpu.VMEM((RW_TCHUNK, 5, C_HEAD_DIM, LANES), F32),
                        pltpu.VMEM((RW_TCHUNK, RW_VROWS, LANES), F32),
                        pltpu.VMEM((RW_TCHUNK, RW_VROWS, LANES), F32)],
        compiler_params=_cparams(("parallel", "arbitrary")),
    )(kl, vl, s0)


def _rwkv_unlanes_kernel(yf_ref, yb_ref, flip_ref, o_ref):
    half = LANES // 2
    flip = flip_ref[...]
    for vl in range(RW_VROWS):
        tf = yf_ref[0, vl].T
        tb = _reverse_lanes(yb_ref[0, vl].T, flip)
        for vh in range(2):
            for b in range(RW_GROUP_BATCH):
                rf = vh * half + b * C_HEADS
                rb = rf + half // 2
                o_ref[0, b, pl.ds((vh * RW_VROWS + vl) * C_HEADS, C_HEADS), :] = (
                    tf[rf:rf + C_HEADS] + tb[rb:rb + C_HEADS])


def rwkv_from_lanes(y, batch, seqlen):
    g = batch // RW_GROUP_BATCH
    tt = RW_RELAYOUT_T
    nj = seqlen // tt
    out = pl.pallas_call(
        _rwkv_unlanes_kernel,
        grid=(g, nj),
        in_specs=[pl.BlockSpec((1, RW_VROWS, tt, LANES), lambda gi, j: (gi, 0, j, 0)),
                  pl.BlockSpec((1, RW_VROWS, tt, LANES), lambda gi, j: (gi, 0, nj - 1 - j, 0)),
                  pl.BlockSpec((tt, tt), lambda gi, j: (0, 0))],
        out_specs=pl.BlockSpec((1, RW_GROUP_BATCH, C_WIDTH, tt), lambda gi, j: (gi, 0, 0, j)),
        out_shape=jax.ShapeDtypeStruct((g, RW_GROUP_BATCH, C_WIDTH, seqlen), F32),
        compiler_params=_cparams(("parallel", "parallel")),
    )(y, y, _exchange(tt))
    return out.reshape(batch, C_WIDTH, seqlen)


def _state_to_lanes(s):
    bsz = s.shape[0]
    g = bsz // RW_GROUP_BATCH
    s = s.reshape(g, RW_GROUP_BATCH, 2, C_HEADS, 2, RW_VROWS, C_HEAD_DIM)
    return jnp.transpose(s, (0, 6, 5, 4, 2, 1, 3)).reshape(g, C_HEAD_DIM, RW_VROWS, LANES)


def _state_from_lanes(s):
    g = s.shape[0]
    s = s.reshape(g, C_HEAD_DIM, RW_VROWS, 2, 2, RW_GROUP_BATCH, C_HEADS)
    s = jnp.transpose(s, (0, 5, 4, 6, 3, 2, 1))
    return s.reshape(g * RW_GROUP_BATCH, 2, C_HEADS, C_HEAD_DIM, C_HEAD_DIM)


def rwkv7_scan_segment(ptf, ptb, s0, off, batch, seqlen):
    kl, vl = rwkv_to_lanes(ptf, ptb, off, batch, seqlen)
    y, sfin = rwkv7_scan_lanes(kl, vl, _state_to_lanes(s0))
    return rwkv_from_lanes(y, batch, seqlen), _state_from_lanes(sfin)


def _merge_kernel(x_ref, oa_ref, ga_ref, gn_ref, yb_ref, yrp_ref, yrs_ref, bonus_ref, rg_ref, lng_ref, lnb_ref,
                  bd_ref, pg_ref, gate_ref, wb_ref, wo_ref, o_ref, *, prompt_tiles):
    oa = oa_ref[0, 0] + oa_ref[1, 0]
    gz = ga_ref[...]
    gn = gn_ref[...]
    parts = []
    for h in range(A_HEADS):
        oh = oa[:, h * A_DV:(h + 1) * A_DV]
        parts.append(oh * lax.rsqrt(jnp.mean(oh * oh, axis=-1, keepdims=True) + EPS) * gn)
    ya = jnp.concatenate(parts, axis=-1) * (gz * jax.nn.sigmoid(gz))
    bd = bd_ref[...]
    yr = jnp.where(pl.program_id(0) < prompt_tiles, yrp_ref[0], yrs_ref[0]).T
    inv = 1.0 / C_HEAD_DIM
    yr = yr - _head_sum(yr, bd) * inv
    var = _head_sum(yr * yr, bd) * inv
    yc = (yr * lax.rsqrt(var + RWKV_GN_EPS) * lng_ref[...] + lnb_ref[...] + bonus_ref[...]) * rg_ref[...]
    pg = pg_ref[...]
    d = D_MODEL
    merged = (jax.nn.sigmoid(pg[:, :d]) * _bdot(ya, wb_ref[0])
              + jax.nn.sigmoid(pg[:, d:2 * d]) * _bdot(yb_ref[...], wb_ref[1])
              + jax.nn.sigmoid(pg[:, 2 * d:]) * _bdot(yc, wb_ref[2]))
    o_ref[...] = x_ref[...] + gate_ref[0] * _bdot(merged, wo_ref[...])


def merge_out(x, o_hgrn, pa, hgrn_norm, yb, yr_prompt, yr_sample, bonus, rw_gate, ln_g, ln_b, pg, gate, wb_bf16,
              wo_bf16, prompt_tiles, tiles_per_sample):
    n, d = x.shape
    bw = BRANCH_WIDTH
    row = lambda i: (i, 0)
    seg = lambda i: (_seg_of_tile(i, prompt_tiles, tiles_per_sample), 0, 0)
    rspec = pl.BlockSpec((ROW_TILE, bw), row)
    vspec = pl.BlockSpec((1, bw), lambda i: (0, 0))
    tiles_per_prompt = yr_prompt.shape[2] // ROW_TILE
    last_p = prompt_tiles - 1

    def prompt_blk(i):
        ic = jnp.minimum(i, last_p)
        return (ic // tiles_per_prompt, 0, ic % tiles_per_prompt)

    def sample_blk(i):
        ic = jnp.maximum(i - prompt_tiles, 0)
        return (ic // tiles_per_sample, 0, ic % tiles_per_sample)

    return pl.pallas_call(
        functools.partial(_merge_kernel, prompt_tiles=prompt_tiles),
        grid=(n // ROW_TILE,),
        in_specs=[pl.BlockSpec((ROW_TILE, d), row),
                  pl.BlockSpec((2, 1, ROW_TILE, bw), lambda i: (0, 0, i, 0)),
                  pl.BlockSpec((ROW_TILE, bw), lambda i: (i, (A_COLS - bw) // bw)),
                  pl.BlockSpec((1, A_DV), lambda i: (0, 0)),
                  rspec,
                  pl.BlockSpec((1, bw, ROW_TILE), prompt_blk),
                  pl.BlockSpec((1, bw, ROW_TILE), sample_blk),
                  rspec, rspec, vspec, vspec,
                  pl.BlockSpec((bw, bw), lambda i: (0, 0)),
                  pl.BlockSpec((ROW_TILE, G_COLS), row),
                  pl.BlockSpec((1, 1, d), seg),
                  pl.BlockSpec((N_BRANCH, bw, d), lambda i: (0, 0, 0)),
                  pl.BlockSpec((d, d), lambda i: (0, 0))],
        out_specs=pl.BlockSpec((ROW_TILE, d), row),
        out_shape=jax.ShapeDtypeStruct((n, d), F32),
        compiler_params=_cparams(("parallel",)),
    )(x, o_hgrn, pa, hgrn_norm.reshape(1, A_DV), yb, yr_prompt, yr_sample, bonus, rw_gate, ln_g.reshape(1, bw),
      ln_b.reshape(1, bw), _head_ones(), pg, gate, wb_bf16, wo_bf16)


PEER_SEL_TILE = 256
PEER_TOK_TILE = 1024
PEER_EXP_TILE = 1024
PEER_RANKS = P_TOPK + 1


def _nt_x3_kernel(a_ref, b_ref, hi_ref, lo_ref):
    ahi, alo = _split_hi_lo(a_ref[...])
    bhi, blo = _split_hi_lo(b_ref[...])
    nt = lambda p, q: lax.dot_general(p, q, (((1,), (1,)), ((), ())), preferred_element_type=F32)
    m = nt(ahi, bhi) + (nt(ahi, blo) + nt(alo, bhi))
    hi, lo = _split_hi_lo(m)
    hi_ref[...] = hi
    lo_ref[...] = lo


def peer_score_matrix(p_wq, p_key1, p_key2):
    d = p_wq.shape[0]
    eye = jnp.eye(P_HEADS, dtype=F32)
    k1 = jnp.einsum('nd,hg->hngd', p_key1, eye)
    k2 = jnp.einsum('nd,hg->hngd', p_key2, eye)
    zeros = jnp.zeros_like(k1)
    kbig = jnp.stack([jnp.stack([k1, zeros], axis=3), jnp.stack([zeros, k2], axis=3)])
    kbig = kbig.reshape(2 * P_HEADS * P_NKEYS, P_HEADS * P_KEY_DIM)
    rows = kbig.shape[0]
    tile = 256
    return pl.pallas_call(
        _nt_x3_kernel,
        grid=(rows // tile,),
        in_specs=[pl.BlockSpec((tile, P_HEADS * P_KEY_DIM), lambda i: (i, 0)),
                  pl.BlockSpec((d, P_HEADS * P_KEY_DIM), lambda i: (0, 0))],
        out_specs=[pl.BlockSpec((tile, d), lambda i: (i, 0))] * 2,
        out_shape=[jax.ShapeDtypeStruct((rows, d), BF16)] * 2,
        compiler_params=_cparams(("parallel",)),
    )(kbig, p_wq)


def _peer_scores_kernel(x_ref, gain_ref, scale_ref, shift_ref, mhi_ref, mlo_ref, ht_ref, s_ref):
    h = _norm_mod(x_ref[...], gain_ref[...], scale_ref[0], shift_ref[0])
    ht = h.T
    hhi, hlo = _split_hi_lo(ht)
    ht_ref[...] = hhi
    mhi = mhi_ref[...]
    s_ref[...] = (jnp.dot(mhi, hhi, preferred_element_type=F32)
                  + (jnp.dot(mhi, hlo, preferred_element_type=F32)
                     + jnp.dot(mlo_ref[...], hhi, preferred_element_type=F32)))


def peer_scores(x, gain, scale, shift, mhi, mlo, prompt_tiles, tiles_per_sample):
    n, d = x.shape
    rows = mhi.shape[0]
    seg = lambda i: (_seg_of_tile(i, prompt_tiles, tiles_per_sample), 0, 0)
    return pl.pallas_call(
        _peer_scores_kernel,
        grid=(n // ROW_TILE,),
        in_specs=[pl.BlockSpec((ROW_TILE, d), lambda i: (i, 0)),
                  pl.BlockSpec((1, d), lambda i: (0, 0)),
                  pl.BlockSpec((1, 1, d), seg),
                  pl.BlockSpec((1, 1, d), seg),
                  pl.BlockSpec((rows, d), lambda i: (0, 0)),
                  pl.BlockSpec((rows, d), lambda i: (0, 0))],
        out_specs=[pl.BlockSpec((d, ROW_TILE), lambda i: (0, i)),
                   pl.BlockSpec((rows, ROW_TILE), lambda i: (0, i))],
        out_shape=[jax.ShapeDtypeStruct((d, n), BF16),
                   jax.ShapeDtypeStruct((rows, n), F32)],
        compiler_params=_cparams(("parallel",)),
    )(x, gain.reshape(1, d), scale, shift, mhi, mlo)


def _peer_select_kernel(s_ref, cnt_ref, e1_ref, rank_ref, e2_ref, v1_scr, v2_scr, st_scr):
    neg = -jnp.inf
    nr = PEER_RANKS
    for h in range(P_HEADS):
        for half, scr in ((0, v1_scr), (1, v2_scr)):
            s = s_ref[half, h]
            m = jnp.max(s, axis=0, keepdims=True)
            scr[0, pl.ds(h, 1), :] = m
            for rnk in range(1, nr):
                m = jnp.max(jnp.where(s < m, s, neg), axis=0, keepdims=True)
                scr[rnk, pl.ds(h, 1), :] = m
    cands = []
    for a in range(nr):
        for b in range(nr):
            if (a + 1) * (b + 1) <= nr:
                cands.append(v1_scr[a] + v2_scr[b])
    top = cands[0]
    m = top
    for rnk in range(1, nr):
        prev = m
        m = None
        for cnd in cands:
            x = jnp.where(cnd < prev, cnd, neg)
            m = x if m is None else jnp.maximum(m, x)
        if rnk == nr - 2:
            t16 = m
    t17 = m
    tau = 0.5 * (t16 + t17)
    zsum = jnp.zeros_like(top)
    for cnd in cands:
        zsum = zsum + jnp.where(cnd > tau, jnp.exp(cnd - top), 0.0)
    st_scr[0] = tau
    st_scr[1] = 1.0 / zsum
    for h in range(P_HEADS):
        s1 = s_ref[0, h]
        s2 = s_ref[1, h]
        thr = st_scr[0, pl.ds(h, 1), :] - s1
        cnt = jnp.zeros_like(s1)
        rank = jnp.zeros_like(s2)
        for b in range(nr):
            v2b = v2_scr[b, pl.ds(h, 1), :]
            cnt = cnt + jnp.where(v2b >= thr, 1.0, 0.0)
            rank = rank + jnp.where(v2b > s2, 1.0, 0.0)
        cnt_ref[h] = cnt
        rank_ref[h] = pltpu.bitcast(rank.astype(BF16), jnp.uint32)
        e1_ref[h] = jnp.exp(s1 - v1_scr[0, pl.ds(h, 1), :])
        e2 = jnp.exp(s2 - v2_scr[0, pl.ds(h, 1), :]) * st_scr[1, pl.ds(h, 1), :]
        e2_ref[h] = pltpu.bitcast(e2.astype(BF16), jnp.uint32)


def peer_select(s4):
    n = s4.shape[-1]
    tl = PEER_SEL_TILE
    ospec = pl.BlockSpec((P_HEADS, P_NKEYS, tl), lambda i: (0, 0, i))
    f32s = jax.ShapeDtypeStruct((P_HEADS, P_NKEYS, n), F32)
    pspec = pl.BlockSpec((P_HEADS, P_NKEYS // 2, tl), lambda i: (0, 0, i))
    packed = jax.ShapeDtypeStruct((P_HEADS, P_NKEYS // 2, n), jnp.uint32)
    return pl.pallas_call(
        _peer_select_kernel,
        grid=(n // tl,),
        in_specs=[pl.BlockSpec((2, P_HEADS, P_NKEYS, tl), lambda i: (0, 0, 0, i))],
        out_specs=[ospec, ospec, pspec, pspec], out_shape=[f32s, f32s, packed, packed],
        scratch_shapes=[pltpu.VMEM((PEER_RANKS, P_HEADS, tl), F32),
                        pltpu.VMEM((PEER_RANKS, P_HEADS, tl), F32),
                        pltpu.VMEM((2, P_HEADS, tl), F32)],
        compiler_params=_cparams(("parallel",)),
    )(s4)


def _peer_expert_kernel(ht_ref, u_ref, vt_ref, cnt_ref, e1_ref, rank_ref, e2_ref, x_ref, gate_ref,
                        o_ref, acc_scr, w_scr, *, n_eb):
    j = pl.program_id(1)

    @pl.when(j == 0)
    def _():
        acc_scr[...] = jnp.zeros_like(acc_scr)

    n_i = PEER_EXP_TILE // P_NKEYS
    sub = 16
    n_jt = P_NKEYS // sub
    for il in range(n_i):
        for lt in range(PEER_TOK_TILE // LANES):
            ls = pl.ds(lt * LANES, LANES)
            w = [None] * n_jt
            for h in range(P_HEADS):
                cb = jnp.broadcast_to(cnt_ref[h, pl.ds(il, 1), ls], (sub, LANES)).astype(BF16)
                eb = jnp.broadcast_to(e1_ref[h, pl.ds(il, 1), ls], (sub, LANES)).astype(BF16)
                for jt in range(n_jt):
                    rs = pl.ds(jt * (sub // 2), sub // 2)
                    rank = pltpu.bitcast(rank_ref[h, rs, ls], BF16)
                    e2 = pltpu.bitcast(e2_ref[h, rs, ls], BF16)
                    sel = jnp.where(rank < cb, e2, jnp.zeros((), BF16)) * eb
                    w[jt] = sel if w[jt] is None else w[jt] + sel
            for jt in range(n_jt):
                w_scr[pl.ds(il * P_NKEYS + jt * sub, sub), ls] = w[jt]
    at = jnp.dot(u_ref[...], ht_ref[...], preferred_element_type=F32)
    p = w_scr[...] * jax.nn.gelu(at).astype(BF16)
    acc_scr[...] += jnp.dot(vt_ref[...], p, preferred_element_type=F32)

    @pl.when(j == n_eb - 1)
    def _():
        o_ref[...] = x_ref[...] + gate_ref[0] * acc_scr[...].T


def peer_experts(ht, u_bf16, vt_bf16, cnt, e1, rank, e2, x, gate, prompt_tiles, tiles_per_sample):
    n, d = x.shape
    tt, eb = PEER_TOK_TILE, PEER_EXP_TILE
    n_eb = P_EXPERTS // eb
    n_i = eb // P_NKEYS
    scale = tt // ROW_TILE
    seg = lambda i, j: (_seg_of_tile(i * scale, prompt_tiles, tiles_per_sample), 0, 0)
    full = pl.BlockSpec((P_HEADS, P_NKEYS // 2, tt), lambda i, j: (0, 0, i))
    part = pl.BlockSpec((P_HEADS, n_i, tt), lambda i, j: (0, j, i))
    return pl.pallas_call(
        functools.partial(_peer_expert_kernel, n_eb=n_eb),
        grid=(n // tt, n_eb),
        in_specs=[pl.BlockSpec((d, tt), lambda i, j: (0, i)),
                  pl.BlockSpec((eb, d), lambda i, j: (j, 0)),
                  pl.BlockSpec((d, eb), lambda i, j: (0, j)),
                  part, part, full, full,
                  pl.BlockSpec((tt, d), lambda i, j: (i, 0)),
                  pl.BlockSpec((1, 1, d), seg)],
        out_specs=pl.BlockSpec((tt, d), lambda i, j: (i, 0)),
        out_shape=jax.ShapeDtypeStruct((n, d), F32),
        scratch_shapes=[pltpu.VMEM((d, tt), F32), pltpu.VMEM((eb, tt), BF16)],
        compiler_params=_cparams(("parallel", "arbitrary")),
    )(ht, u_bf16, vt_bf16, cnt, e1, rank, e2, x, gate)


def _final_norm_kernel(x_ref, g_ref, o_ref):
    x = x_ref[...]
    o_ref[...] = x * lax.rsqrt(jnp.mean(x * x, axis=-1, keepdims=True) + EPS) * g_ref[...]


def final_norm(x, gain):
    n, d = x.shape
    return pl.pallas_call(
        _final_norm_kernel,
        grid=(n // ROW_TILE,),
        in_specs=[pl.BlockSpec((ROW_TILE, d), lambda i: (i, 0)), pl.BlockSpec((1, d), lambda i: (0, 0))],
        out_specs=pl.BlockSpec((ROW_TILE, d), lambda i: (i, 0)),
        out_shape=jax.ShapeDtypeStruct((n, d), F32),
        compiler_params=_cparams(("parallel",)),
    )(x, gain.reshape(1, d))


def trunk(x_prompt, x_sample, state_hgrn, state_rwkv, c, c_ctx, params):
    bp, tp, d = x_prompt.shape
    bs, ts, _ = x_sample.shape
    n_p, n_s = bp * tp, bs * ts
    prompt_tiles, tiles_per_sample = n_p // ROW_TILE, ts // ROW_TILE
    tiling = (prompt_tiles, tiles_per_sample)
    x = jnp.concatenate([x_prompt.reshape(n_p, d), x_sample.reshape(n_s, d)], axis=0)
    cvec = jnp.concatenate([c_ctx[None], c], axis=0)

    lb_p = jax.nn.softmax(params['hgrn_lb_logits'], axis=1)
    lb = jnp.cumsum(lb_p, axis=1)
    lb = lb - lb[:, :1]
    tables = {tp: dft_tables(tp), ts: dft_tables(ts)}
    zero_a = jnp.zeros((bp, 2, A_HEADS, A_DK, A_DV), F32)
    zero_c = jnp.zeros((bp, 2, C_HEADS, C_HEAD_DIM, C_HEAD_DIM), F32)

    new_a, new_c = [], []
    for l in range(DEPTH):
        lp = {k: v[l] for k, v in params.items() if k not in ('hgrn_lb_logits', 'final_norm')}
        mod = jnp.dot(jax.nn.silu(cvec), lp['w_ada'], precision=lax.Precision.HIGHEST) + lp['b_ada']
        sh1, sc1, g1, sh2, sc2, g2 = [m[:, None, :] for m in jnp.split(mod, 6, axis=-1)]

        w_in = lp['w_in'].astype(BF16)
        splits = np.cumsum([0, A_COLS, B_COLS, C_COLS, G_COLS])
        rp = rwkv_permuted_params(lp, w_in[:, splits[2]:splits[3]])
        w_parts = [w_in[:, splits[0]:splits[1]], w_in[:, splits[1]:splits[2]], rp['w_c'], w_in[:, splits[3]:]]
        pa, pb, pc, pg = [norm_mod_matmul(x, lp['norm1'], sc1, sh1, w, *tiling) for w in w_parts]

        lb_l = lb[:, l][:, None, :]
        oa_p, sa_p = hgrn2_scan(pa, lb_l, zero_a, 0, bp, tp)
        oa_s, _ = hgrn2_scan(pa, lb_l, state_hgrn[:, l], n_p, bs, ts)
        o_hgrn = jnp.concatenate([oa_p, oa_s], axis=2)

        yb = jnp.concatenate([hyena_branch(pb[:n_p].reshape(bp, tp, B_COLS), lp, tables[tp]).reshape(n_p, -1),
                              hyena_branch(pb[n_p:].reshape(bs, ts, B_COLS), lp, tables[ts]).reshape(n_s, -1)])

        ptf, ptb, rw_gate, bonus = rwkv_pre(pc, rp, prompt_tiles, tp // ROW_TILE, tiles_per_sample, tp, ts)
        yr_p, sc_p = rwkv7_scan_segment(ptf, ptb, zero_c, 0, bp, tp)
        yr_s, _ = rwkv7_scan_segment(ptf, ptb, state_rwkv[:, l], n_p, bs, ts)

        wb = jnp.stack([lp['w_branch'][0], lp['w_branch'][1], rp['w_branch_c']]).astype(BF16)
        x = merge_out(x, o_hgrn, pa, lp['hgrn_norm'], yb, yr_p, yr_s, bonus, rw_gate, rp['rw_ln_g'], rp['rw_ln_b'],
                      pg, g1, wb, lp['w_out'].astype(BF16), *tiling)

        mhi, mlo = peer_score_matrix(lp['p_wq'], lp['p_key1'], lp['p_key2'])
        ht, st = peer_scores(x, lp['norm2'], sc2, sh2, mhi, mlo, *tiling)
        s4 = st.reshape(2, P_HEADS, P_NKEYS, n_p + n_s)
        cnt, e1, rank, e2 = peer_select(s4)
        x = peer_experts(ht, lp['p_u'].astype(BF16), lp['p_v'].T.astype(BF16), cnt, e1, rank, e2, x, g2, *tiling)

        new_a.append(sa_p)
        new_c.append(sc_p)

    y = final_norm(x, params['final_norm'])
    return (y[:n_p].reshape(bp, tp, d), y[n_p:].reshape(bs, ts, d),
            jnp.stack(new_a, axis=1), jnp.stack(new_c, axis=1))


def kernel(x_prompt, x_sample, state_hgrn, state_rwkv, c, c_ctx, w_ada, b_ada, norm1, norm2, w_in,
           hgrn_lb_logits, hgrn_norm, hy_conv_w, hy_conv_b, hy_w1, hy_b1, hy_freq1, hy_w2, hy_b2,
           hy_freq2, hy_w3, hy_delta, hy_bias, rw_mu, rw_w0, rw_w2, rw_a0, rw_a2, rw_g2, rw_kk, rw_ka,
           rw_rk, rw_ln_g, rw_ln_b, w_branch, w_out, p_wq, p_key1, p_key2, p_u, p_v, final_norm):
    params = dict(w_ada=w_ada, b_ada=b_ada, norm1=norm1, norm2=norm2, w_in=w_in,
                  hgrn_lb_logits=hgrn_lb_logits, hgrn_norm=hgrn_norm, hy_conv_w=hy_conv_w,
                  hy_conv_b=hy_conv_b, hy_w1=hy_w1, hy_b1=hy_b1, hy_freq1=hy_freq1, hy_w2=hy_w2,
                  hy_b2=hy_b2, hy_freq2=hy_freq2, hy_w3=hy_w3, hy_delta=hy_delta, hy_bias=hy_bias,
                  rw_mu=rw_mu, rw_w0=rw_w0, rw_w2=rw_w2, rw_a0=rw_a0, rw_a2=rw_a2, rw_g2=rw_g2,
                  rw_kk=rw_kk, rw_ka=rw_ka, rw_rk=rw_rk, rw_ln_g=rw_ln_g, rw_ln_b=rw_ln_b,
                  w_branch=w_branch, w_out=w_out, p_wq=p_wq, p_key1=p_key1, p_key2=p_key2,
                  p_u=p_u, p_v=p_v, final_norm=final_norm)
    return trunk(x_prompt, x_sample, state_hgrn, state_rwkv, c, c_ctx, params)
```

```python
import functools
import math
import jax
import jax.numpy as jnp
from jax import lax
import numpy as np
from jax.experimental import pallas as pl
from jax.experimental.pallas import tpu as pltpu

D_MODEL = 1024
DEPTH = 4
GRID_W = 64
BRANCH_WIDTH = 512
N_BRANCH = 3

A_DK = 128
A_DV = 128
A_HEADS = BRANCH_WIDTH // A_DV

HY_BANDS = 16
HY_EMB = 1 + 2 * HY_BANDS
HY_FFN = 64

C_HEAD_DIM = 64
C_HEADS = BRANCH_WIDTH // C_HEAD_DIM
C_WIDTH = BRANCH_WIDTH
C_DECAY_LORA = 64
C_AAA_LORA = 64
C_GATE_LORA = 128

A_COLS = 3 * A_HEADS * A_DK + A_HEADS * A_DV + BRANCH_WIDTH
B_COLS = 3 * BRANCH_WIDTH
C_COLS = 3 * C_WIDTH + 2 * C_DECAY_LORA + 2 * C_AAA_LORA + C_GATE_LORA
G_COLS = N_BRANCH * D_MODEL

P_HEADS = 8
P_NKEYS = 128
P_EXPERTS = P_NKEYS * P_NKEYS
P_TOPK = 16
P_KEY_DIM = 128

EPS = 1e-6
RWKV_GN_EPS = 64e-5

LANES = 128
ROW_TILE = 256
VMEM_LIMIT = 56 * 1024 * 1024

F32 = jnp.float32
BF16 = jnp.bfloat16


def _cparams(sem):
    return pltpu.CompilerParams(dimension_semantics=sem, vmem_limit_bytes=VMEM_LIMIT)


def _bdot(a, b):
    return jnp.dot(a.astype(BF16), b.astype(BF16), preferred_element_type=F32)


def _split_hi_lo(x):
    hi = x.astype(BF16)
    lo = (x - hi.astype(F32)).astype(BF16)
    return hi, lo


def _seg_of_tile(i, prompt_tiles, tiles_per_sample):
    return jnp.where(i < prompt_tiles, 0, 1 + (i - prompt_tiles) // tiles_per_sample)


def _norm_mod(x, gain, scale, shift):
    return x * lax.rsqrt(jnp.mean(x * x, axis=-1, keepdims=True) + EPS) * gain * (1.0 + scale) + shift


def _inproj_kernel(x_ref, gain_ref, scale_ref, shift_ref, w_ref, o_ref):
    h = _norm_mod(x_ref[...], gain_ref[...], scale_ref[0], shift_ref[0])
    o_ref[...] = jnp.dot(h.astype(BF16), w_ref[...], preferred_element_type=F32)


def norm_mod_matmul(x, gain, scale, shift, w_bf16, prompt_tiles, tiles_per_sample):
    n, d = x.shape
    cols = w_bf16.shape[1]
    seg = lambda i: (_seg_of_tile(i, prompt_tiles, tiles_per_sample), 0, 0)
    return pl.pallas_call(
        _inproj_kernel,
        grid=(n // ROW_TILE,),
        in_specs=[pl.BlockSpec((ROW_TILE, d), lambda i: (i, 0)),
                  pl.BlockSpec((1, d), lambda i: (0, 0)),
                  pl.BlockSpec((1, 1, d), seg),
                  pl.BlockSpec((1, 1, d), seg),
                  pl.BlockSpec((d, cols), lambda i: (0, 0))],
        out_specs=pl.BlockSpec((ROW_TILE, cols), lambda i: (i, 0)),
        out_shape=jax.ShapeDtypeStruct((n, cols), F32),
        compiler_params=_cparams(("parallel",)),
    )(x, gain.reshape(1, d), scale, shift, w_bf16)


HG_CHUNK = 128
HG_LEVELS = 7


def _hgrn_tables():
    c = HG_CHUNK
    out = np.zeros((2, HG_LEVELS + 2, c, c), np.float32)
    t = np.arange(c)
    for d in range(2):
        pos = t if d == 0 else c - 1 - t
        pt, pu = pos[:, None], pos[None, :]
        for lv in range(HG_LEVELS):
            m = c >> (lv + 1)
            same = (pt // m) == (pu // m)
            query = ((pt // m) % 2) == 1
            out[d, lv] = same & np.where(query, pu <= pt, pu > pt)
        out[d, HG_LEVELS] = pu <= pt
        out[d, HG_LEVELS + 1] = pu > pt
    return out.reshape(2, (HG_LEVELS + 2) * c, c)


def _hgrn_kernel(q_ref, fz_ref, v_ref, lb_ref, tab_ref, s0_ref, o_ref, sfin_ref, st_scr, *, n_chunks):
    c = HG_CHUNK
    d = pl.program_id(1)
    ci = pl.program_id(2)

    @pl.when(ci == 0)
    def _():
        for h in range(A_HEADS):
            st_scr[h] = s0_ref[0, 0, h].T

    row = lax.broadcasted_iota(jnp.int32, (c, LANES), 0)
    pos_r = row + d * (c - 1 - 2 * row)
    ti = lax.broadcasted_iota(jnp.int32, (c, c), 0)
    si = lax.broadcasted_iota(jnp.int32, (c, c), 1)
    pos_t = ti + d * (c - 1 - 2 * ti)
    pos_s = si + d * (c - 1 - 2 * si)
    tab = tab_ref[0]

    for h in range(A_HEADS):
        hs = pl.ds(h * LANES, LANES)
        qz = q_ref[:, hs]
        q = qz * jax.nn.sigmoid(qz)
        lb = lb_ref[0, :, hs]
        f = lb + (1.0 - lb) * jax.nn.sigmoid(fz_ref[:, hs])
        kd = 1.0 - f
        logf = jnp.log(f)
        v = v_ref[:, hs]

        lhi, llo = _split_hi_lo(logf)
        args = (jnp.dot(tab, lhi, preferred_element_type=F32)
                + jnp.dot(tab, llo, preferred_element_type=F32))
        e_all = jnp.exp(args)

        attn = jnp.where(ti == si, jnp.sum(q * kd, axis=-1, keepdims=True), 0.0)
        for lv in range(HG_LEVELS):
            sh = HG_LEVELS - 1 - lv
            e = e_all[lv * c:(lv + 1) * c]
            is_q = ((pos_r >> sh) & 1) == 1
            qs = jnp.where(is_q, q * e, 0.0).astype(BF16)
            ks = jnp.where(is_q, 0.0, kd * e).astype(BF16)
            sc = lax.dot_general(qs, ks, (((1,), (1,)), ((), ())), preferred_element_type=F32)
            attn = attn + jnp.where((pos_t >> (sh + 1)) == (pos_s >> (sh + 1)), sc, 0.0)

        e_cum = e_all[HG_LEVELS * c:(HG_LEVELS + 1) * c]
        e_rev = e_all[(HG_LEVELS + 1) * c:]
        st = st_scr[h]
        o = _bdot(attn, v) + lax.dot_general((q * e_cum).astype(BF16), st.astype(BF16),
                                             (((1,), (1,)), ((), ())), preferred_element_type=F32)
        o_ref[0, 0, :, hs] = o
        dec = jnp.exp(jnp.sum(logf, axis=0, keepdims=True))
        st_new = st * dec + _bdot(v.T, kd * e_rev)
        st_scr[h] = st_new

        @pl.when(ci == n_chunks - 1)
        def _():
            sfin_ref[0, 0, h] = st_new.T


def hgrn2_scan(pa, lb, s0, row_off, batch, seqlen):
    c = HG_CHUNK
    n_chunks = seqlen // c
    base = row_off // c
    hw = A_HEADS * A_DK
    tab = jnp.asarray(_hgrn_tables(), BF16)

    def rows(b, ci, d):
        return base + b * n_chunks + ci + d * (n_chunks - 1 - 2 * ci)

    def orow(b, ci, d):
        return b * n_chunks + ci + d * (n_chunks - 1 - 2 * ci)

    return pl.pallas_call(
        functools.partial(_hgrn_kernel, n_chunks=n_chunks),
        grid=(batch, 2, n_chunks),
        in_specs=[pl.BlockSpec((c, hw), lambda b, d, ci: (rows(b, ci, d), 0)),
                  pl.BlockSpec((c, hw), lambda b, d, ci: (rows(b, ci, d), 1 + d)),
                  pl.BlockSpec((c, hw), lambda b, d, ci: (rows(b, ci, d), 3)),
                  pl.BlockSpec((1, 1, hw), lambda b, d, ci: (d, 0, 0)),
                  pl.BlockSpec((1, (HG_LEVELS + 2) * c, c), lambda b, d, ci: (d, 0, 0)),
                  pl.BlockSpec((1, 1, A_HEADS, A_DK, A_DV), lambda b, d, ci: (b, d, 0, 0, 0))],
        out_specs=[pl.BlockSpec((1, 1, c, hw), lambda b, d, ci: (d, 0, orow(b, ci, d), 0)),
                   pl.BlockSpec((1, 1, A_HEADS, A_DK, A_DV), lambda b, d, ci: (b, d, 0, 0, 0))],
        out_shape=[jax.ShapeDtypeStruct((2, 1, batch * seqlen, A_HEADS * A_DV), F32),
                   jax.ShapeDtypeStruct((batch, 2, A_HEADS, A_DK, A_DV), F32)],
        scratch_shapes=[pltpu.VMEM((A_HEADS, A_DV, A_DK), F32)],
        compiler_params=_cparams(("parallel", "parallel", "arbitrary")),
    )(pa, pa, pa, lb, tab, s0)


def _dft_table_kernel(fc_ref, fs_ref, gc_ref, gs_ref, *, length, tile):
    n2 = 2 * length
    a = lax.broadcasted_iota(jnp.int32, (tile, length), 0) + pl.program_id(0) * tile
    b = lax.broadcasted_iota(jnp.int32, (tile, length), 1)
    ang = ((a * b) & (n2 - 1)).astype(F32) * (2.0 * math.pi / n2)
    co, si = jnp.cos(ang), jnp.sin(ang)
    alt_b = (1 - 2 * (b & 1)).astype(F32)
    alt_a = (1 - 2 * (a & 1)).astype(F32)
    fc_ref[...] = co.astype(BF16)
    fs_ref[...] = jnp.where(a == 0, alt_b, -si).astype(BF16)
    wk = jnp.where(b == 0, 1.0 / n2, 2.0 / n2)
    gc_ref[...] = (wk * co).astype(BF16)
    gs_ref[...] = jnp.where(b == 0, alt_a * (1.0 / n2), -wk * si).astype(BF16)


def dft_tables(length):
    tile = min(length, 256)
    spec = pl.BlockSpec((tile, length), lambda i: (i, 0))
    shp = jax.ShapeDtypeStruct((length, length), BF16)
    return pl.pallas_call(
        functools.partial(_dft_table_kernel, length=length, tile=tile),
        grid=(length // tile,),
        out_specs=[spec] * 4, out_shape=[shp] * 4,
        compiler_params=_cparams(("parallel",)),
    )()


def _dft_fwd_kernel(fc_ref, fs_ref, z_ref, zr_ref, zi_ref):
    z = z_ref[0].astype(BF16)
    zr_ref[0] = jnp.dot(fc_ref[...], z, preferred_element_type=F32)
    zi_ref[0] = jnp.dot(fs_ref[...], z, preferred_element_type=F32)


def dft_forward(fc, fs, z):
    batch, length, ch = z.shape
    tile = min(length, 512)
    return pl.pallas_call(
        _dft_fwd_kernel,
        grid=(length // tile, batch),
        in_specs=[pl.BlockSpec((tile, length), lambda i, b: (i, 0)),
                  pl.BlockSpec((tile, length), lambda i, b: (i, 0)),
                  pl.BlockSpec((1, length, ch), lambda i, b: (b, 0, 0))],
        out_specs=[pl.BlockSpec((1, tile, ch), lambda i, b: (b, i, 0))] * 2,
        out_shape=[jax.ShapeDtypeStruct((batch, length, ch), F32)] * 2,
        compiler_params=_cparams(("parallel", "arbitrary")),
    )(fc, fs, z)


def _spec_mul_kernel(zr_ref, zi_ref, kr_ref, ki_ref, yr_ref, yi_ref, *, tile):
    row = lax.broadcasted_iota(jnp.int32, (tile, 1), 0) + pl.program_id(1) * tile
    sgn = (1 - 2 * (row & 1)).astype(F32)
    kr = kr_ref[0] + sgn * kr_ref[1]
    ki = ki_ref[0] + sgn * ki_ref[1]
    zr, zi = zr_ref[0], zi_ref[0]
    packed = row == 0
    yr_ref[0] = jnp.where(packed, zr * kr, zr * kr - zi * ki).astype(BF16)
    yi_ref[0] = jnp.where(packed, zi * ki, zr * ki + zi * kr).astype(BF16)


def spectrum_multiply(zr, zi, kr, ki):
    batch, length, ch = zr.shape
    tile = min(length, 512)
    zspec = pl.BlockSpec((1, tile, ch), lambda b, i: (b, i, 0))
    kspec = pl.BlockSpec((2, tile, ch), lambda b, i: (0, i, 0))
    return pl.pallas_call(
        functools.partial(_spec_mul_kernel, tile=tile),
        grid=(batch, length // tile),
        in_specs=[zspec, zspec, kspec, kspec],
        out_specs=[zspec, zspec],
        out_shape=[jax.ShapeDtypeStruct((batch, length, ch), BF16)] * 2,
        compiler_params=_cparams(("parallel", "parallel")),
    )(zr, zi, kr, ki)


def _dft_inv_kernel(gc_ref, gs_ref, yr_ref, yi_ref, x0_ref, z_ref, bias_ref, o_ref):
    y = (jnp.dot(gc_ref[...], yr_ref[0], preferred_element_type=F32)
         + jnp.dot(gs_ref[...], yi_ref[0], preferred_element_type=F32))
    o_ref[0] = x0_ref[0] * (y + z_ref[0] * bias_ref[...])


def dft_inverse_gate(gc, gs, yr, yi, x0, z, bias):
    batch, length, ch = yr.shape
    tile = min(length, 512)
    tspec = pl.BlockSpec((1, tile, ch), lambda i, b: (b, i, 0))
    return pl.pallas_call(
        _dft_inv_kernel,
        grid=(length // tile, batch),
        in_specs=[pl.BlockSpec((tile, length), lambda i, b: (i, 0)),
                  pl.BlockSpec((tile, length), lambda i, b: (i, 0)),
                  pl.BlockSpec((1, length, ch), lambda i, b: (b, 0, 0)),
                  pl.BlockSpec((1, length, ch), lambda i, b: (b, 0, 0)),
                  tspec, tspec,
                  pl.BlockSpec((1, ch), lambda i, b: (0, 0))],
        out_specs=tspec,
        out_shape=jax.ShapeDtypeStruct((batch, length, ch), F32),
        compiler_params=_cparams(("parallel", "arbitrary")),
    )(gc, gs, yr, yi, x0, z, bias.reshape(1, ch))


def shift_prev(x):
    return jnp.pad(x, ((0, 0), (1, 0), (0, 0)))[:, :-1]


def shift_next(x):
    return jnp.pad(x, ((0, 0), (0, 1), (0, 0)))[:, 1:]


def hyena_filter(length, lp):
    hp = lax.Precision.HIGHEST
    t = jnp.linspace(0.0, 1.0, length, dtype=F32)[:, None]
    bands = jnp.linspace(1e-4, HY_BANDS - 1, HY_BANDS, dtype=F32)[None, :]
    ang = (2 * math.pi / length) * jnp.arange(length, dtype=F32)[:, None] * bands
    z = jnp.concatenate([t, jnp.cos(ang), -jnp.sin(ang)], axis=-1)
    hf = jnp.sin(lp['hy_freq1'] * (jnp.dot(z, lp['hy_w1'], precision=hp) + lp['hy_b1']))
    hf = jnp.sin(lp['hy_freq2'] * (jnp.dot(hf, lp['hy_w2'], precision=hp) + lp['hy_b2']))
    hf = jnp.dot(hf, lp['hy_w3'], precision=hp).reshape(length, 2, BRANCH_WIDTH)
    hf = hf * jnp.exp(-t * jnp.abs(lp['hy_delta']))[:, None, :]
    lo = hf[:, 0]
    hi = jnp.concatenate([jnp.zeros((1, BRANCH_WIDTH), F32), hf[:0:-1, 1]], axis=0)
    norm = jnp.sum(jnp.abs(lo), axis=0, keepdims=True) + jnp.sum(jnp.abs(hi), axis=0, keepdims=True)
    return jnp.stack([lo, hi]) / norm


def hyena_branch(pb, lp, tables):
    batch, length, _ = pb.shape
    fc, fs, gc, gs = tables
    cw = lp['hy_conv_w']
    u = shift_prev(pb) * cw[0] + pb * cw[1] + shift_next(pb) * cw[2] + lp['hy_conv_b']
    x0, x1, v = jnp.split(u, 3, axis=-1)
    z = v * x1
    kr, ki = dft_forward(fc, fs, hyena_filter(length, lp))
    zr, zi = dft_forward(fc, fs, z)
    yr, yi = spectrum_multiply(zr, zi, kr, ki)
    return dft_inverse_gate(gc, gs, yr, yi, x0, z, lp['hy_bias'])


RW_GROUP_BATCH = 4
RW_VROWS = C_HEAD_DIM // 2
RW_TCHUNK = 32
RW_STACK = 6


def _head_sum(y, bd):
    hi, lo = _split_hi_lo(y)
    return jnp.dot(hi, bd, preferred_element_type=F32) + jnp.dot(lo, bd, preferred_element_type=F32)


def _split3(x):
    hi = x.astype(BF16)
    r1 = x - hi.astype(F32)
    mid = r1.astype(BF16)
    return hi, mid, (r1 - mid.astype(F32)).astype(BF16)


def _reverse_lanes(x, flip):
    hi, mid, lo = _split3(x)
    dot = lambda p: jnp.dot(p, flip, preferred_element_type=F32)
    return (dot(hi) + dot(mid)) + dot(lo)


def _exchange(n):
    return jnp.asarray(np.eye(n)[::-1], BF16)


def _head_ones():
    h = np.arange(C_WIDTH) % C_HEADS
    return jnp.asarray(h[:, None] == h[None, :], BF16)


RW_PERM = np.arange(C_WIDTH).reshape(C_HEADS, C_HEAD_DIM).T.reshape(-1)


def rwkv_permuted_params(lp, w_c):
    p = RW_PERM
    cols = np.concatenate([p, C_WIDTH + p, 2 * C_WIDTH + p, np.arange(3 * C_WIDTH, C_COLS)])
    return dict(w_c=w_c[:, cols], rw_mu=lp['rw_mu'][cols], rw_kk=lp['rw_kk'][p], rw_ka=lp['rw_ka'][p],
                rw_rk=lp['rw_rk'].reshape(-1)[p], rw_w0=lp['rw_w0'][:, p], rw_w2=lp['rw_w2'][:, :, p],
                rw_a0=lp['rw_a0'][:, p], rw_a2=lp['rw_a2'][:, :, p], rw_g2=lp['rw_g2'][:, p],
                rw_ln_g=lp['rw_ln_g'][p], rw_ln_b=lp['rw_ln_b'][p], w_branch_c=lp['w_branch'][2][p])


def _rwkv_pre_kernel(prv_ref, cur_ref, nxt_ref, mu_ref, kks_ref, ka_ref, rk_ref, w0_ref, w2_ref, a0_ref,
                     a2_ref, g2_ref, bd_ref, flip_ref, *out_refs, prompt_tiles, tiles_per_prompt, tiles_per_sample,
                     prompt_len, sample_len):
    i = pl.program_id(0)
    tm = ROW_TILE
    latent = i >= prompt_tiles
    tile_in_seq = jnp.where(latent, (i - prompt_tiles) % tiles_per_sample, i % tiles_per_prompt)
    seq_len = jnp.where(latent, sample_len, prompt_len)
    row = lax.broadcasted_iota(jnp.int32, (tm, 1), 0)
    pos = tile_in_seq * tm + row
    col = pos % GRID_W
    one = lambda c: jnp.where(c, 1.0, 0.0)
    m_left = jnp.where(latent, one(col != 0), one(pos != 0))
    m_right = jnp.where(latent, one(col != GRID_W - 1), one(pos != seq_len - 1))
    m_up = one(pos >= GRID_W)
    m_down = one(pos < seq_len - GRID_W)
    c_lr = jnp.where(latent, 0.25, 0.5)
    c_ud = jnp.where(latent, 0.25, 0.0)

    cur = cur_ref[...]
    prv = prv_ref[...]
    nxt = nxt_ref[...]
    left = jnp.where(row == 0, prv[tm - 1:tm], pltpu.roll(cur, 1, axis=0))
    right = jnp.where(row == tm - 1, nxt[0:1], pltpu.roll(cur, tm - 1, axis=0))
    up = jnp.concatenate([prv[tm - GRID_W:], cur[:tm - GRID_W]], axis=0)
    down = jnp.concatenate([cur[GRID_W:], nxt[:GRID_W]], axis=0)
    nm = c_lr * (left * m_left + right * m_right) + c_ud * (up * m_up + down * m_down)
    x = cur + (nm - cur) * mu_ref[...]

    cw = C_WIDTH
    r, k, v = x[:, :cw], x[:, cw:2 * cw], x[:, 2 * cw:3 * cw]
    o = 3 * cw
    wd = x[:, o:o + 2 * C_DECAY_LORA]
    ad = x[:, o + 2 * C_DECAY_LORA:o + 2 * C_DECAY_LORA + 2 * C_AAA_LORA]
    gd = x[:, o + 2 * C_DECAY_LORA + 2 * C_AAA_LORA:]
    bd = bd_ref[...]

    kk = k * kks_ref[...]
    kkn = kk * lax.rsqrt(_head_sum(kk * kk, bd) + 1e-12)
    ptf_o, ptb_o, g_o, bonus_o = out_refs
    flip = flip_ref[...]
    for idx, arr in enumerate((r, v, -kkn)):
        at = arr.T
        ptf_o[idx] = at
        ptb_o[idx] = _reverse_lanes(at, flip)
    g_o[...] = _bdot(jax.nn.sigmoid(gd), g2_ref[...])
    bonus_o[...] = _head_sum(r * k * rk_ref[...], bd) * v
    for d, pt_o in enumerate((ptf_o, ptb_o)):
        z = -(w0_ref[d] + _bdot(jnp.tanh(wd[:, d * C_DECAY_LORA:(d + 1) * C_DECAY_LORA]), w2_ref[d]))
        softplus = jnp.maximum(z, 0.0) + jnp.log(1.0 + jnp.exp(-jnp.abs(z)))
        ag = jax.nn.sigmoid(a0_ref[d] + _bdot(ad[:, d * C_AAA_LORA:(d + 1) * C_AAA_LORA], a2_ref[d]))
        outs = (jnp.exp(-jnp.exp(-softplus - 0.5)), k * (1.0 + (ag - 1.0) * ka_ref[...]), kkn * ag)
        for idx, arr in enumerate(outs):
            at = arr.T
            pt_o[3 + idx] = at if d == 0 else _reverse_lanes(at, flip)


def rwkv_pre(pc, rp, prompt_tiles, tiles_per_prompt, tiles_per_sample, prompt_len, sample_len):
    n, cc = pc.shape
    cw = C_WIDTH
    nt = n // ROW_TILE
    vec = lambda a: a.reshape(1, -1)
    const2 = lambda shp: pl.BlockSpec(shp, lambda i: (0, 0))
    const3 = lambda shp: pl.BlockSpec(shp, lambda i: (0, 0, 0))
    ospec = pl.BlockSpec((ROW_TILE, cw), lambda i: (i, 0))

    def mirrored(i):
        per = jnp.where(i < prompt_tiles, tiles_per_prompt, tiles_per_sample)
        rel = jnp.where(i < prompt_tiles, i, i - prompt_tiles)
        return i - 2 * (rel % per) + per - 1

    kern = functools.partial(_rwkv_pre_kernel, prompt_tiles=prompt_tiles, tiles_per_prompt=tiles_per_prompt,
                             tiles_per_sample=tiles_per_sample, prompt_len=prompt_len, sample_len=sample_len)
    return pl.pallas_call(
        kern,
        grid=(nt,),
        in_specs=[pl.BlockSpec((ROW_TILE, cc), lambda i: (jnp.maximum(i - 1, 0), 0)),
                  pl.BlockSpec((ROW_TILE, cc), lambda i: (i, 0)),
                  pl.BlockSpec((ROW_TILE, cc), lambda i: (jnp.minimum(i + 1, nt - 1), 0)),
                  const2((1, cc)), const2((1, cw)), const2((1, cw)), const2((1, cw)),
                  const3((2, 1, cw)), const3((2, C_DECAY_LORA, cw)),
                  const3((2, 1, cw)), const3((2, C_AAA_LORA, cw)),
                  const2((C_GATE_LORA, cw)), const2((cw, cw)), const2((ROW_TILE, ROW_TILE))],
        out_specs=[pl.BlockSpec((RW_STACK, cw, ROW_TILE), lambda i: (0, 0, i)),
                   pl.BlockSpec((RW_STACK, cw, ROW_TILE), lambda i: (0, 0, mirrored(i))), ospec, ospec],
        out_shape=[jax.ShapeDtypeStruct((RW_STACK, cw, n), F32), jax.ShapeDtypeStruct((RW_STACK, cw, n), F32),
                   jax.ShapeDtypeStruct((n, cw), F32), jax.ShapeDtypeStruct((n, cw), F32)],
        compiler_params=_cparams(("parallel",)),
    )(pc, pc, pc, vec(rp['rw_mu']), vec(rp['rw_kk']), vec(rp['rw_ka']), vec(rp['rw_rk']),
      rp['rw_w0'][:, None, :], rp['rw_w2'].astype(BF16), rp['rw_a0'][:, None, :], rp['rw_a2'].astype(BF16),
      rp['rw_g2'].astype(BF16), _head_ones(), _exchange(ROW_TILE))


RW_RELAYOUT_T = 128
RW_KQ = 4
RW_KROWS = (0, 3, 4, 2, 5)


def _rwkv_lanes_kernel(f0, f1, f2, f3, b0, b1, b2, b3, kl_ref, vl_ref):
    q = pl.program_id(2)
    stacks = ((f0, f1, f2, f3), (b0, b1, b2, b3))
    nk = C_HEAD_DIM // RW_KQ
    for kl in range(nk):
        r0 = pl.multiple_of((q * nk + kl) * C_HEADS, C_HEADS)
        for ai, src in enumerate(RW_KROWS):
            base = [st[b][src, pl.ds(r0, C_HEADS), :] for st in stacks for b in range(RW_GROUP_BATCH)]
            kl_ref[0, ai, kl] = jnp.concatenate(base + base, axis=0).T
    nv = RW_VROWS // RW_KQ
    for vi in range(nv):
        pieces = []
        for vh in range(2):
            r0 = pl.multiple_of((vh * RW_VROWS + q * nv + vi) * C_HEADS, C_HEADS)
            pieces += [st[b][1, pl.ds(r0, C_HEADS), :] for st in stacks for b in range(RW_GROUP_BATCH)]
        vl_ref[0, vi] = jnp.concatenate(pieces, axis=0).T


def rwkv_to_lanes(ptf, ptb, off, batch, seqlen):
    g = batch // RW_GROUP_BATCH
    tt = RW_RELAYOUT_T
    nj = seqlen // tt
    col0 = off // tt
    nk, nv = C_HEAD_DIM // RW_KQ, RW_VROWS // RW_KQ

    def src(b):
        return pl.BlockSpec((RW_STACK, C_WIDTH, tt),
                            lambda gi, j, q: (0, 0, col0 + (gi * RW_GROUP_BATCH + b) * nj + j))

    specs = [src(b) for b in range(RW_GROUP_BATCH)]
    return pl.pallas_call(
        _rwkv_lanes_kernel,
        grid=(g, nj, RW_KQ),
        in_specs=specs + specs,
        out_specs=[pl.BlockSpec((1, 5, nk, tt, LANES), lambda gi, j, q: (gi, 0, q, j, 0)),
                   pl.BlockSpec((1, nv, tt, LANES), lambda gi, j, q: (gi, q, j, 0))],
        out_shape=[jax.ShapeDtypeStruct((g, 5, C_HEAD_DIM, seqlen, LANES), F32),
                   jax.ShapeDtypeStruct((g, RW_VROWS, seqlen, LANES), F32)],
        compiler_params=_cparams(("parallel", "parallel", "arbitrary")),
    )(ptf, ptf, ptf, ptf, ptb, ptb, ptb, ptb)


def _rwkv_scan_kernel(k_ref, v_ref, s0_ref, y_ref, sfin_ref, s_scr, row_scr, v_scr, y_scr, *, n_tc):
    tc = pl.program_id(1)

    @pl.when(tc == 0)
    def _():
        s_scr[...] = s0_ref[0]

    nk = C_HEAD_DIM
    R, W, KD, A, B = range(5)
    for t0 in range(0, RW_TCHUNK, 8):
        for a in range(5):
            for k0 in range(0, nk, 8):
                row_scr[pl.ds(t0, 8), a, pl.ds(k0, 8), :] = jnp.swapaxes(
                    k_ref[0, a, pl.ds(k0, 8), pl.ds(t0, 8), :], 0, 1)
        for r0 in range(0, RW_VROWS, 8):
            v_scr[pl.ds(t0, 8), pl.ds(r0, 8), :] = jnp.swapaxes(v_ref[0, pl.ds(r0, 8), pl.ds(t0, 8), :], 0, 1)

    def step(t, carry):
        parts = [jnp.zeros((RW_VROWS, LANES), F32) for _ in range(4)]
        for kk in range(nk):
            parts[kk % 4] = parts[kk % 4] + s_scr[kk] * row_scr[t, A, pl.ds(kk, 1), :]
        sa = (parts[0] + parts[1]) + (parts[2] + parts[3])
        v_t = v_scr[t]
        ys = [jnp.zeros((RW_VROWS, LANES), F32) for _ in range(4)]
        for kk in range(nk):
            s_new = (s_scr[kk] * row_scr[t, W, pl.ds(kk, 1), :] + sa * row_scr[t, B, pl.ds(kk, 1), :]
                     + v_t * row_scr[t, KD, pl.ds(kk, 1), :])
            s_scr[kk] = s_new
            ys[kk % 4] = ys[kk % 4] + s_new * row_scr[t, R, pl.ds(kk, 1), :]
        y_scr[t] = (ys[0] + ys[1]) + (ys[2] + ys[3])
        return carry

    lax.fori_loop(0, RW_TCHUNK, step, 0)
    for t0 in range(0, RW_TCHUNK, 8):
        for r0 in range(0, RW_VROWS, 8):
            y_ref[0, pl.ds(r0, 8), pl.ds(t0, 8), :] = jnp.swapaxes(y_scr[pl.ds(t0, 8), pl.ds(r0, 8), :], 0, 1)

    @pl.when(tc == n_tc - 1)
    def _():
        sfin_ref[0] = s_scr[...]


def rwkv7_scan_lanes(kl, vl, s0):
    g, t = vl.shape[0], vl.shape[2]
    n_tc = t // RW_TCHUNK
    kspec = pl.BlockSpec((1, 5, C_HEAD_DIM, RW_TCHUNK, LANES), lambda gi, ti: (gi, 0, 0, ti, 0))
    vspec = pl.BlockSpec((1, RW_VROWS, RW_TCHUNK, LANES), lambda gi, ti: (gi, 0, ti, 0))
    sspec = pl.BlockSpec((1, C_HEAD_DIM, RW_VROWS, LANES), lambda gi, ti: (gi, 0, 0, 0))
    return pl.pallas_call(
        functools.partial(_rwkv_scan_kernel, n_tc=n_tc),
        grid=(g, n_tc),
        in_specs=[kspec, vspec, sspec],
        out_specs=[vspec, sspec],
        out_shape=[jax.ShapeDtypeStruct((g, RW_VROWS, t, LANES), F32),
                   jax.ShapeDtypeStruct((g, C_HEAD_DIM, RW_VROWS, LANES), F32)],
        scratch_shapes=[pltpu.VMEM((C_HEAD_DIM, RW_VROWS, LANES), F32),
                        pltpu.VMEM((RW_TCHUNK, 5, C_HEAD_DIM, LANES), F32),
                        pltpu.VMEM((RW_TCHUNK, RW_VROWS, LANES), F32),
                        pltpu.VMEM((RW_TCHUNK, RW_VROWS, LANES), F32)],
        compiler_params=_cparams(("parallel", "arbitrary")),
    )(kl, vl, s0)


def _rwkv_unlanes_kernel(yf_ref, yb_ref, flip_ref, o_ref):
    half = LANES // 2
    flip = flip_ref[...]
    for vl in range(RW_VROWS):
        tf = yf_ref[0, vl].T
        tb = _reverse_lanes(yb_ref[0, vl].T, flip)
        for vh in range(2):
            for b in range(RW_GROUP_BATCH):
                rf = vh * half + b * C_HEADS
                rb = rf + half // 2
                o_ref[0, b, pl.ds((vh * RW_VROWS + vl) * C_HEADS, C_HEADS), :] = (
                    tf[rf:rf + C_HEADS] + tb[rb:rb + C_HEADS])


def rwkv_from_lanes(y, batch, seqlen):
    g = batch // RW_GROUP_BATCH
    tt = RW_RELAYOUT_T
    nj = seqlen // tt
    out = pl.pallas_call(
        _rwkv_unlanes_kernel,
        grid=(g, nj),
        in_specs=[pl.BlockSpec((1, RW_VROWS, tt, LANES), lambda gi, j: (gi, 0, j, 0)),
                  pl.BlockSpec((1, RW_VROWS, tt, LANES), lambda gi, j: (gi, 0, nj - 1 - j, 0)),
                  pl.BlockSpec((tt, tt), lambda gi, j: (0, 0))],
        out_specs=pl.BlockSpec((1, RW_GROUP_BATCH, C_WIDTH, tt), lambda gi, j: (gi, 0, 0, j)),
        out_shape=jax.ShapeDtypeStruct((g, RW_GROUP_BATCH, C_WIDTH, seqlen), F32),
        compiler_params=_cparams(("parallel", "parallel")),
    )(y, y, _exchange(tt))
    return out.reshape(batch, C_WIDTH, seqlen)


def _state_to_lanes(s):
    bsz = s.shape[0]
    g = bsz // RW_GROUP_BATCH
    s = s.reshape(g, RW_GROUP_BATCH, 2, C_HEADS, 2, RW_VROWS, C_HEAD_DIM)
    return jnp.transpose(s, (0, 6, 5, 4, 2, 1, 3)).reshape(g, C_HEAD_DIM, RW_VROWS, LANES)


def _state_from_lanes(s):
    g = s.shape[0]
    s = s.reshape(g, C_HEAD_DIM, RW_VROWS, 2, 2, RW_GROUP_BATCH, C_HEADS)
    s = jnp.transpose(s, (0, 5, 4, 6, 3, 2, 1))
    return s.reshape(g * RW_GROUP_BATCH, 2, C_HEADS, C_HEAD_DIM, C_HEAD_DIM)


def rwkv7_scan_segment(ptf, ptb, s0, off, batch, seqlen):
    kl, vl = rwkv_to_lanes(ptf, ptb, off, batch, seqlen)
    y, sfin = rwkv7_scan_lanes(kl, vl, _state_to_lanes(s0))
    return rwkv_from_lanes(y, batch, seqlen), _state_from_lanes(sfin)


def _merge_kernel(x_ref, oap_ref, oas_ref, ga_ref, gn_ref, ybp_ref, ybs_ref, yrp_ref, yrs_ref, bonus_ref, rg_ref,
                  lng_ref, lnb_ref, bd_ref, pg_ref, gate_ref, wb_ref, wo_ref, o_ref, *, prompt_tiles):
    is_prompt = pl.program_id(0) < prompt_tiles
    oa = jnp.where(is_prompt, oap_ref[0, 0] + oap_ref[1, 0], oas_ref[0, 0] + oas_ref[1, 0])
    yb = jnp.where(is_prompt, ybp_ref[...], ybs_ref[...])
    gz = ga_ref[...]
    gn = gn_ref[...]
    parts = []
    for h in range(A_HEADS):
        oh = oa[:, h * A_DV:(h + 1) * A_DV]
        parts.append(oh * lax.rsqrt(jnp.mean(oh * oh, axis=-1, keepdims=True) + EPS) * gn)
    ya = jnp.concatenate(parts, axis=-1) * (gz * jax.nn.sigmoid(gz))
    bd = bd_ref[...]
    yr = jnp.where(is_prompt, yrp_ref[0], yrs_ref[0]).T
    inv = 1.0 / C_HEAD_DIM
    yr = yr - _head_sum(yr, bd) * inv
    var = _head_sum(yr * yr, bd) * inv
    yc = (yr * lax.rsqrt(var + RWKV_GN_EPS) * lng_ref[...] + lnb_ref[...] + bonus_ref[...]) * rg_ref[...]
    pg = pg_ref[...]
    d = D_MODEL
    merged = (jax.nn.sigmoid(pg[:, :d]) * _bdot(ya, wb_ref[0])
              + jax.nn.sigmoid(pg[:, d:2 * d]) * _bdot(yb, wb_ref[1])
              + jax.nn.sigmoid(pg[:, 2 * d:]) * _bdot(yc, wb_ref[2]))
    o_ref[...] = x_ref[...] + gate_ref[0] * _bdot(merged, wo_ref[...])


def merge_out(x, oa_prompt, oa_sample, pa, hgrn_norm, yb_prompt, yb_sample, yr_prompt, yr_sample, bonus, rw_gate,
              ln_g, ln_b, pg, gate, wb_bf16, wo_bf16, prompt_tiles, tiles_per_sample):
    n, d = x.shape
    bw = BRANCH_WIDTH
    row = lambda i: (i, 0)
    seg = lambda i: (_seg_of_tile(i, prompt_tiles, tiles_per_sample), 0, 0)
    rspec = pl.BlockSpec((ROW_TILE, bw), row)
    vspec = pl.BlockSpec((1, bw), lambda i: (0, 0))
    tiles_per_prompt = yr_prompt.shape[2] // ROW_TILE
    last_p = prompt_tiles - 1

    def prompt_blk(i):
        ic = jnp.minimum(i, last_p)
        return (ic // tiles_per_prompt, 0, ic % tiles_per_prompt)

    def sample_blk(i):
        ic = jnp.maximum(i - prompt_tiles, 0)
        return (ic // tiles_per_sample, 0, ic % tiles_per_sample)

    return pl.pallas_call(
        functools.partial(_merge_kernel, prompt_tiles=prompt_tiles),
        grid=(n // ROW_TILE,),
        in_specs=[pl.BlockSpec((ROW_TILE, d), row),
                  pl.BlockSpec((2, 1, ROW_TILE, bw), lambda i: (0, 0, jnp.minimum(i, last_p), 0)),
                  pl.BlockSpec((2, 1, ROW_TILE, bw), lambda i: (0, 0, jnp.maximum(i - prompt_tiles, 0), 0)),
                  pl.BlockSpec((ROW_TILE, bw), lambda i: (i, (A_COLS - bw) // bw)),
                  pl.BlockSpec((1, A_DV), lambda i: (0, 0)),
                  pl.BlockSpec((ROW_TILE, bw), lambda i: (jnp.minimum(i, last_p), 0)),
                  pl.BlockSpec((ROW_TILE, bw), lambda i: (jnp.maximum(i - prompt_tiles, 0), 0)),
                  pl.BlockSpec((1, bw, ROW_TILE), prompt_blk),
                  pl.BlockSpec((1, bw, ROW_TILE), sample_blk),
                  rspec, rspec, vspec, vspec,
                  pl.BlockSpec((bw, bw), lambda i: (0, 0)),
                  pl.BlockSpec((ROW_TILE, G_COLS), row),
                  pl.BlockSpec((1, 1, d), seg),
                  pl.BlockSpec((N_BRANCH, bw, d), lambda i: (0, 0, 0)),
                  pl.BlockSpec((d, d), lambda i: (0, 0))],
        out_specs=pl.BlockSpec((ROW_TILE, d), row),
        out_shape=jax.ShapeDtypeStruct((n, d), F32),
        compiler_params=_cparams(("parallel",)),
    )(x, oa_prompt, oa_sample, pa, hgrn_norm.reshape(1, A_DV), yb_prompt, yb_sample, yr_prompt, yr_sample, bonus,
      rw_gate, ln_g.reshape(1, bw), ln_b.reshape(1, bw), _head_ones(), pg, gate, wb_bf16, wo_bf16)


PEER_SEL_TILE = 256
PEER_TOK_TILE = 1024
PEER_EXP_TILE = 1024
PEER_RANKS = P_TOPK + 1


def _nt_x3_kernel(a_ref, b_ref, hi_ref, lo_ref):
    ahi, alo = _split_hi_lo(a_ref[...])
    bhi, blo = _split_hi_lo(b_ref[...])
    nt = lambda p, q: lax.dot_general(p, q, (((1,), (1,)), ((), ())), preferred_element_type=F32)
    m = nt(ahi, bhi) + (nt(ahi, blo) + nt(alo, bhi))
    hi, lo = _split_hi_lo(m)
    hi_ref[...] = hi
    lo_ref[...] = lo


def peer_score_matrix(p_wq, p_key1, p_key2):
    d = p_wq.shape[0]
    eye = jnp.eye(P_HEADS, dtype=F32)
    k1 = jnp.einsum('nd,hg->hngd', p_key1, eye)
    k2 = jnp.einsum('nd,hg->hngd', p_key2, eye)
    zeros = jnp.zeros_like(k1)
    kbig = jnp.stack([jnp.stack([k1, zeros], axis=3), jnp.stack([zeros, k2], axis=3)])
    kbig = kbig.reshape(2 * P_HEADS * P_NKEYS, P_HEADS * P_KEY_DIM)
    rows = kbig.shape[0]
    tile = 256
    return pl.pallas_call(
        _nt_x3_kernel,
        grid=(rows // tile,),
        in_specs=[pl.BlockSpec((tile, P_HEADS * P_KEY_DIM), lambda i: (i, 0)),
                  pl.BlockSpec((d, P_HEADS * P_KEY_DIM), lambda i: (0, 0))],
        out_specs=[pl.BlockSpec((tile, d), lambda i: (i, 0))] * 2,
        out_shape=[jax.ShapeDtypeStruct((rows, d), BF16)] * 2,
        compiler_params=_cparams(("parallel",)),
    )(kbig, p_wq)


def _peer_scores_kernel(x_ref, gain_ref, scale_ref, shift_ref, mhi_ref, mlo_ref, ht_ref, s_ref):
    h = _norm_mod(x_ref[...], gain_ref[...], scale_ref[0], shift_ref[0])
    ht = h.T
    hhi, hlo = _split_hi_lo(ht)
    ht_ref[...] = hhi
    mhi = mhi_ref[...]
    s_ref[...] = (jnp.dot(mhi, hhi, preferred_element_type=F32)
                  + (jnp.dot(mhi, hlo, preferred_element_type=F32)
                     + jnp.dot(mlo_ref[...], hhi, preferred_element_type=F32)))


def peer_scores(x, gain, scale, shift, mhi, mlo, prompt_tiles, tiles_per_sample):
    n, d = x.shape
    rows = mhi.shape[0]
    seg = lambda i: (_seg_of_tile(i, prompt_tiles, tiles_per_sample), 0, 0)
    return pl.pallas_call(
        _peer_scores_kernel,
        grid=(n // ROW_TILE,),
        in_specs=[pl.BlockSpec((ROW_TILE, d), lambda i: (i, 0)),
                  pl.BlockSpec((1, d), lambda i: (0, 0)),
                  pl.BlockSpec((1, 1, d), seg),
                  pl.BlockSpec((1, 1, d), seg),
                  pl.BlockSpec((rows, d), lambda i: (0, 0)),
                  pl.BlockSpec((rows, d), lambda i: (0, 0))],
        out_specs=[pl.BlockSpec((d, ROW_TILE), lambda i: (0, i)),
                   pl.BlockSpec((rows, ROW_TILE), lambda i: (0, i))],
        out_shape=[jax.ShapeDtypeStruct((d, n), BF16),
                   jax.ShapeDtypeStruct((rows, n), F32)],
        compiler_params=_cparams(("parallel",)),
    )(x, gain.reshape(1, d), scale, shift, mhi, mlo)


def _peer_select_kernel(s_ref, cnt_ref, e1_ref, rank_ref, e2_ref, v1_scr, v2_scr, st_scr):
    neg = -jnp.inf
    nr = PEER_RANKS
    for h in range(P_HEADS):
        for half, scr in ((0, v1_scr), (1, v2_scr)):
            s = s_ref[half, h]
            m = jnp.max(s, axis=0, keepdims=True)
            scr[0, pl.ds(h, 1), :] = m
            for rnk in range(1, nr):
                m = jnp.max(jnp.where(s < m, s, neg), axis=0, keepdims=True)
                scr[rnk, pl.ds(h, 1), :] = m
    cands = []
    for a in range(nr):
        for b in range(nr):
            if (a + 1) * (b + 1) <= nr:
                cands.append(v1_scr[a] + v2_scr[b])
    top = cands[0]
    m = top
    for rnk in range(1, nr):
        prev = m
        m = None
        for cnd in cands:
            x = jnp.where(cnd < prev, cnd, neg)
            m = x if m is None else jnp.maximum(m, x)
        if rnk == nr - 2:
            t16 = m
    t17 = m
    tau = 0.5 * (t16 + t17)
    zsum = jnp.zeros_like(top)
    for cnd in cands:
        zsum = zsum + jnp.where(cnd > tau, jnp.exp(cnd - top), 0.0)
    st_scr[0] = tau
    st_scr[1] = 1.0 / zsum
    for h in range(P_HEADS):
        s1 = s_ref[0, h]
        s2 = s_ref[1, h]
        thr = st_scr[0, pl.ds(h, 1), :] - s1
        cnt = jnp.zeros_like(s1)
        rank = jnp.zeros_like(s2)
        for b in range(nr):
            v2b = v2_scr[b, pl.ds(h, 1), :]
            cnt = cnt + jnp.where(v2b >= thr, 1.0, 0.0)
            rank = rank + jnp.where(v2b > s2, 1.0, 0.0)
        cnt_ref[h] = cnt
        rank_ref[h] = pltpu.bitcast(rank.astype(BF16), jnp.uint32)
        e1_ref[h] = jnp.exp(s1 - v1_scr[0, pl.ds(h, 1), :])
        e2 = jnp.exp(s2 - v2_scr[0, pl.ds(h, 1), :]) * st_scr[1, pl.ds(h, 1), :]
        e2_ref[h] = pltpu.bitcast(e2.astype(BF16), jnp.uint32)


def peer_select(s4):
    n = s4.shape[-1]
    tl = PEER_SEL_TILE
    ospec = pl.BlockSpec((P_HEADS, P_NKEYS, tl), lambda i: (0, 0, i))
    f32s = jax.ShapeDtypeStruct((P_HEADS, P_NKEYS, n), F32)
    pspec = pl.BlockSpec((P_HEADS, P_NKEYS // 2, tl), lambda i: (0, 0, i))
    packed = jax.ShapeDtypeStruct((P_HEADS, P_NKEYS // 2, n), jnp.uint32)
    return pl.pallas_call(
        _peer_select_kernel,
        grid=(n // tl,),
        in_specs=[pl.BlockSpec((2, P_HEADS, P_NKEYS, tl), lambda i: (0, 0, 0, i))],
        out_specs=[ospec, ospec, pspec, pspec], out_shape=[f32s, f32s, packed, packed],
        scratch_shapes=[pltpu.VMEM((PEER_RANKS, P_HEADS, tl), F32),
                        pltpu.VMEM((PEER_RANKS, P_HEADS, tl), F32),
                        pltpu.VMEM((2, P_HEADS, tl), F32)],
        compiler_params=_cparams(("parallel",)),
    )(s4)


def _peer_expert_kernel(ht_ref, u_ref, vt_ref, cnt_ref, e1_ref, rank_ref, e2_ref, x_ref, gate_ref,
                        o_ref, acc_scr, w_scr, *, n_eb):
    j = pl.program_id(1)

    @pl.when(j == 0)
    def _():
        acc_scr[...] = jnp.zeros_like(acc_scr)

    n_i = PEER_EXP_TILE // P_NKEYS
    sub = 16
    n_jt = P_NKEYS // sub
    for il in range(n_i):
        for lt in range(PEER_TOK_TILE // LANES):
            ls = pl.ds(lt * LANES, LANES)
            w = [None] * n_jt
            for h in range(P_HEADS):
                cb = jnp.broadcast_to(cnt_ref[h, pl.ds(il, 1), ls], (sub, LANES)).astype(BF16)
                eb = jnp.broadcast_to(e1_ref[h, pl.ds(il, 1), ls], (sub, LANES)).astype(BF16)
                for jt in range(n_jt):
                    rs = pl.ds(jt * (sub // 2), sub // 2)
                    rank = pltpu.bitcast(rank_ref[h, rs, ls], BF16)
                    e2 = pltpu.bitcast(e2_ref[h, rs, ls], BF16)
                    sel = jnp.where(rank < cb, e2, jnp.zeros((), BF16)) * eb
                    w[jt] = sel if w[jt] is None else w[jt] + sel
            for jt in range(n_jt):
                w_scr[pl.ds(il * P_NKEYS + jt * sub, sub), ls] = w[jt]
    at = jnp.dot(u_ref[...], ht_ref[...], preferred_element_type=F32)
    p = w_scr[...] * jax.nn.gelu(at).astype(BF16)
    acc_scr[...] += jnp.dot(vt_ref[...], p, preferred_element_type=F32)

    @pl.when(j == n_eb - 1)
    def _():
        o_ref[...] = x_ref[...] + gate_ref[0] * acc_scr[...].T


def peer_experts(ht, u_bf16, vt_bf16, cnt, e1, rank, e2, x, gate, prompt_tiles, tiles_per_sample):
    n, d = x.shape
    tt, eb = PEER_TOK_TILE, PEER_EXP_TILE
    n_eb = P_EXPERTS // eb
    n_i = eb // P_NKEYS
    scale = tt // ROW_TILE
    seg = lambda i, j: (_seg_of_tile(i * scale, prompt_tiles, tiles_per_sample), 0, 0)
    full = pl.BlockSpec((P_HEADS, P_NKEYS // 2, tt), lambda i, j: (0, 0, i))
    part = pl.BlockSpec((P_HEADS, n_i, tt), lambda i, j: (0, j, i))
    return pl.pallas_call(
        functools.partial(_peer_expert_kernel, n_eb=n_eb),
        grid=(n // tt, n_eb),
        in_specs=[pl.BlockSpec((d, tt), lambda i, j: (0, i)),
                  pl.BlockSpec((eb, d), lambda i, j: (j, 0)),
                  pl.BlockSpec((d, eb), lambda i, j: (0, j)),
                  part, part, full, full,
                  pl.BlockSpec((tt, d), lambda i, j: (i, 0)),
                  pl.BlockSpec((1, 1, d), seg)],
        out_specs=pl.BlockSpec((tt, d), lambda i, j: (i, 0)),
        out_shape=jax.ShapeDtypeStruct((n, d), F32),
        scratch_shapes=[pltpu.VMEM((d, tt), F32), pltpu.VMEM((eb, tt), BF16)],
        compiler_params=_cparams(("parallel", "arbitrary")),
    )(ht, u_bf16, vt_bf16, cnt, e1, rank, e2, x, gate)


def _final_norm_kernel(x_ref, g_ref, o_ref):
    x = x_ref[...]
    o_ref[...] = x * lax.rsqrt(jnp.mean(x * x, axis=-1, keepdims=True) + EPS) * g_ref[...]


def final_norm(x, gain):
    n, d = x.shape
    return pl.pallas_call(
        _final_norm_kernel,
        grid=(n // ROW_TILE,),
        in_specs=[pl.BlockSpec((ROW_TILE, d), lambda i: (i, 0)), pl.BlockSpec((1, d), lambda i: (0, 0))],
        out_specs=pl.BlockSpec((ROW_TILE, d), lambda i: (i, 0)),
        out_shape=jax.ShapeDtypeStruct((n, d), F32),
        compiler_params=_cparams(("parallel",)),
    )(x, gain.reshape(1, d))


def trunk(x_prompt, x_sample, state_hgrn, state_rwkv, c, c_ctx, params):
    bp, tp, d = x_prompt.shape
    bs, ts, _ = x_sample.shape
    n_p, n_s = bp * tp, bs * ts
    prompt_tiles, tiles_per_sample = n_p // ROW_TILE, ts // ROW_TILE
    tiling = (prompt_tiles, tiles_per_sample)
    x = jnp.concatenate([x_prompt.reshape(n_p, d), x_sample.reshape(n_s, d)], axis=0)
    cvec = jnp.concatenate([c_ctx[None], c], axis=0)

    lb_p = jax.nn.softmax(params['hgrn_lb_logits'], axis=1)
    lb = jnp.cumsum(lb_p, axis=1)
    lb = lb - lb[:, :1]
    tables = {tp: dft_tables(tp), ts: dft_tables(ts)}
    zero_a = jnp.zeros((bp, 2, A_HEADS, A_DK, A_DV), F32)
    zero_c = jnp.zeros((bp, 2, C_HEADS, C_HEAD_DIM, C_HEAD_DIM), F32)

    new_a, new_c = [], []
    for l in range(DEPTH):
        lp = {k: v[l] for k, v in params.items() if k not in ('hgrn_lb_logits', 'final_norm')}
        mod = jnp.dot(jax.nn.silu(cvec), lp['w_ada'], precision=lax.Precision.HIGHEST) + lp['b_ada']
        sh1, sc1, g1, sh2, sc2, g2 = [m[:, None, :] for m in jnp.split(mod, 6, axis=-1)]

        w_in = lp['w_in'].astype(BF16)
        splits = np.cumsum([0, A_COLS, B_COLS, C_COLS, G_COLS])
        rp = rwkv_permuted_params(lp, w_in[:, splits[2]:splits[3]])
        w_parts = [w_in[:, splits[0]:splits[1]], w_in[:, splits[1]:splits[2]], rp['w_c'], w_in[:, splits[3]:]]
        pa, pb, pc, pg = [norm_mod_matmul(x, lp['norm1'], sc1, sh1, w, *tiling) for w in w_parts]

        lb_l = lb[:, l][:, None, :]
        oa_p, sa_p = hgrn2_scan(pa, lb_l, zero_a, 0, bp, tp)
        oa_s, _ = hgrn2_scan(pa, lb_l, state_hgrn[:, l], n_p, bs, ts)

        yb_p = hyena_branch(pb[:n_p].reshape(bp, tp, B_COLS), lp, tables[tp]).reshape(n_p, -1)
        yb_s = hyena_branch(pb[n_p:].reshape(bs, ts, B_COLS), lp, tables[ts]).reshape(n_s, -1)

        ptf, ptb, rw_gate, bonus = rwkv_pre(pc, rp, prompt_tiles, tp // ROW_TILE, tiles_per_sample, tp, ts)
        yr_p, sc_p = rwkv7_scan_segment(ptf, ptb, zero_c, 0, bp, tp)
        yr_s, _ = rwkv7_scan_segment(ptf, ptb, state_rwkv[:, l], n_p, bs, ts)

        wb = jnp.stack([lp['w_branch'][0], lp['w_branch'][1], rp['w_branch_c']]).astype(BF16)
        x = merge_out(x, oa_p, oa_s, pa, lp['hgrn_norm'], yb_p, yb_s, yr_p, yr_s, bonus, rw_gate, rp['rw_ln_g'],
                      rp['rw_ln_b'], pg, g1, wb, lp['w_out'].astype(BF16), *tiling)

        mhi, mlo = peer_score_matrix(lp['p_wq'], lp['p_key1'], lp['p_key2'])
        ht, st = peer_scores(x, lp['norm2'], sc2, sh2, mhi, mlo, *tiling)
        s4 = st.reshape(2, P_HEADS, P_NKEYS, n_p + n_s)
        cnt, e1, rank, e2 = peer_select(s4)
        x = peer_experts(ht, lp['p_u'].astype(BF16), lp['p_v'].T.astype(BF16), cnt, e1, rank, e2, x, g2, *tiling)

        new_a.append(sa_p)
        new_c.append(sc_p)

    y = final_norm(x, params['final_norm'])
    return (y[:n_p].reshape(bp, tp, d), y[n_p:].reshape(bs, ts, d),
            jnp.stack(new_a, axis=1), jnp.stack(new_c, axis=1))


def kernel(x_prompt, x_sample, state_hgrn, state_rwkv, c, c_ctx, w_ada, b_ada, norm1, norm2, w_in,
           hgrn_lb_logits, hgrn_norm, hy_conv_w, hy_conv_b, hy_w1, hy_b1, hy_freq1, hy_w2, hy_b2,
           hy_freq2, hy_w3, hy_delta, hy_bias, rw_mu, rw_w0, rw_w2, rw_a0, rw_a2, rw_g2, rw_kk, rw_ka,
           rw_rk, rw_ln_g, rw_ln_b, w_branch, w_out, p_wq, p_key1, p_key2, p_u, p_v, final_norm):
    params = dict(w_ada=w_ada, b_ada=b_ada, norm1=norm1, norm2=norm2, w_in=w_in,
                  hgrn_lb_logits=hgrn_lb_logits, hgrn_norm=hgrn_norm, hy_conv_w=hy_conv_w,
                  hy_conv_b=hy_conv_b, hy_w1=hy_w1, hy_b1=hy_b1, hy_freq1=hy_freq1, hy_w2=hy_w2,
                  hy_b2=hy_b2, hy_freq2=hy_freq2, hy_w3=hy_w3, hy_delta=hy_delta, hy_bias=hy_bias,
                  rw_mu=rw_mu, rw_w0=rw_w0, rw_w2=rw_w2, rw_a0=rw_a0, rw_a2=rw_a2, rw_g2=rw_g2,
                  rw_kk=rw_kk, rw_ka=rw_ka, rw_rk=rw_rk, rw_ln_g=rw_ln_g, rw_ln_b=rw_ln_b,
                  w_branch=w_branch, w_out=w_out, p_wq=p_wq, p_key1=p_key1, p_key2=p_key2,
                  p_u=p_u, p_v=p_v, final_norm=final_norm)
    return trunk(x_prompt, x_sample, state_hgrn, state_rwkv, c, c_ctx, params)
```
